```python
import math
import jax, jax.numpy as jnp
from jax import lax
import numpy as np

D_MODEL = 4096
BATCH = 4
SEQ = 2048
DEPTH = 4
DEC_BATCH = 128
DEC_SEQ = 8
PAST_LEN = 16384
PAGE_SIZE = 128

F32 = jnp.float32
EPS = 1e-5
N_MIXERS = 3
N_LAYERS_A = (DEPTH + 2) // 3
N_LAYERS_B = (DEPTH + 1) // 3
N_LAYERS_C = DEPTH // 3

S5_GROUP = 16
S5_GROUPS = D_MODEL // S5_GROUP
S5_STATE = 64
S5_DT_MIN = 1e-3
S5_DT_MAX = 1e-1

SSD_INNER = 2 * D_MODEL
SSD_HEAD_DIM = 64
SSD_HEADS = SSD_INNER // SSD_HEAD_DIM
SSD_GROUPS = 8
SSD_HPG = SSD_HEADS // SSD_GROUPS
SSD_STATE = 128
SSD_CONV = 4
SSD_GN = SSD_GROUPS * SSD_STATE
SSD_CONV_DIM = SSD_INNER + 2 * SSD_GN
SSD_IN_DIM = SSD_INNER + SSD_CONV_DIM + SSD_HEADS
SSD_CHUNK = 128
SSD_DT_MIN = 1e-3
SSD_DT_MAX = 1e-1

GLA_HEADS = 8
GLA_DK = D_MODEL // 2 // GLA_HEADS
GLA_DV = D_MODEL // GLA_HEADS
GLA_KEY = GLA_HEADS * GLA_DK
GLA_VAL = GLA_HEADS * GLA_DV
GLA_RANK = 16
GLA_TAU = 16.0
GLA_CHUNK = 64
GLA_IN_DIM = 2 * GLA_KEY + 2 * GLA_VAL + GLA_RANK

MEM_LEN = 256
XA_HEADS = 4
XA_HEAD_DIM = 128
XA_DIM = XA_HEADS * XA_HEAD_DIM

MLP_HIDDEN = 4 * D_MODEL

kernel_name = 'hybrid_s5_ssd_gla_memxattn_step'


def _rmsnorm(x, g):
    xf = x.astype(F32)
    y = xf * lax.rsqrt(jnp.mean(xf * xf, axis=-1, keepdims=True) + EPS)
    return (y * g.astype(F32)).astype(x.dtype)


def _causal_mask(q):
    return jnp.tril(jnp.ones((q, q), dtype=bool))


def _to_chunks(a, n_chunks):
    b, l = a.shape[:2]
    a = a.reshape((b, n_chunks, l // n_chunks) + a.shape[2:])
    return jnp.moveaxis(a, 1, 0)


def _from_chunks(a):
    a = jnp.moveaxis(a, 0, 1)
    return a.reshape((a.shape[0], a.shape[1] * a.shape[2]) + a.shape[3:])


def _complex_affine_combine(e1, e2):
    a1r, a1i, b1r, b1i = e1
    a2r, a2i, b2r, b2i = e2
    return (a2r * a1r - a2i * a1i,
            a2r * a1i + a2i * a1r,
            a2r * b1r - a2i * b1i + b2r,
            a2r * b1i + a2i * b1r + b2i)


def _s5_mixer(u, h_re, h_im, a_re, a_im, log_dt, b_re, b_im, c_re, c_im, d_skip, w1, w2):
    bsz, L, _ = u.shape
    uf = u.astype(F32)
    ug = uf.reshape(bsz, L, S5_GROUPS, S5_GROUP)
    lam_re = a_re.astype(F32)
    lam_im = a_im.astype(F32)
    dt = jnp.exp(log_dt.astype(F32))[:, None]
    mag = jnp.exp(lam_re * dt)
    ang = lam_im * dt
    ab_re = mag * jnp.cos(ang)
    ab_im = mag * jnp.sin(ang)
    den = lam_re * lam_re + lam_im * lam_im
    f_re = ((ab_re - 1.0) * lam_re + ab_im * lam_im) / den
    f_im = (ab_im * lam_re - (ab_re - 1.0) * lam_im) / den
    br = b_re.astype(F32)
    bi = b_im.astype(F32)
    bb_re = f_re[..., None] * br - f_im[..., None] * bi
    bb_im = f_re[..., None] * bi + f_im[..., None] * br
    bu_re = jnp.einsum('blgc,gpc->lbgp', ug, bb_re)
    bu_im = jnp.einsum('blgc,gpc->lbgp', ug, bb_im)
    h_re = h_re.astype(F32)
    h_im = h_im.astype(F32)
    bu_re = bu_re.at[0].add(ab_re * h_re - ab_im * h_im)
    bu_im = bu_im.at[0].add(ab_re * h_im + ab_im * h_re)
    a_seq_re = jnp.broadcast_to(ab_re, (L, 1) + ab_re.shape)
    a_seq_im = jnp.broadcast_to(ab_im, (L, 1) + ab_im.shape)
    _, _, s_re, s_im = lax.associative_scan(
        _complex_affine_combine, (a_seq_re, a_seq_im, bu_re, bu_im), axis=0)
    y = (jnp.einsum('lbgp,gcp->blgc', s_re, c_re.astype(F32))
         - jnp.einsum('lbgp,gcp->blgc', s_im, c_im.astype(F32)))
    y = y.reshape(bsz, L, D_MODEL) + d_skip.astype(F32) * uf
    act = jax.nn.gelu(y).astype(u.dtype)
    out = (act @ w1) * jax.nn.sigmoid(act @ w2)
    return out, s_re[-1], s_im[-1]


def _ssd_mixer(u, ssm_state, conv_state, w_in, conv_w, conv_b, dt_bias, a_log, d_skip, norm_g, w_out):
    bsz, L, _ = u.shape
    proj = u @ w_in
    z = proj[..., :SSD_INNER].astype(F32)
    xbc = proj[..., SSD_INNER:SSD_INNER + SSD_CONV_DIM].astype(F32)
    dt_raw = proj[..., SSD_INNER + SSD_CONV_DIM:].astype(F32)
    padded = jnp.concatenate([conv_state.astype(F32), xbc], axis=1)
    new_conv = padded[:, L:]
    cw = conv_w.astype(F32)
    conv = conv_b.astype(F32)
    for k in range(SSD_CONV):
        conv = conv + padded[:, k:k + L] * cw[k]
    xbc = jax.nn.silu(conv)
    xs = xbc[..., :SSD_INNER].reshape(bsz, L, SSD_GROUPS, SSD_HPG, SSD_HEAD_DIM)
    bm = xbc[..., SSD_INNER:SSD_INNER + SSD_GN].reshape(bsz, L, SSD_GROUPS, SSD_STATE)
    cm = xbc[..., SSD_INNER + SSD_GN:].reshape(bsz, L, SSD_GROUPS, SSD_STATE)
    dt = jax.nn.softplus(dt_raw + dt_bias.astype(F32)).reshape(bsz, L, SSD_GROUPS, SSD_HPG)
    a = -jnp.exp(a_log.astype(F32)).reshape(SSD_GROUPS, SSD_HPG)
    q = math.gcd(L, SSD_CHUNK)
    nc = L // q
    mask = _causal_mask(q)[None, :, :, None, None]

    def step(state, inp):
        xc, bc, cc, dtc = inp
        cum = jnp.cumsum(dtc * a, axis=1)
        seg = cum[:, :, None] - cum[:, None, :]
        decay = jnp.exp(jnp.where(mask, seg, -jnp.inf))
        cb = jnp.einsum('bign,bjgn->bijg', cc, bc)
        w = cb[..., None] * decay * dtc[:, None]
        y = jnp.einsum('bijgr,bjgrp->bigrp', w, xc)
        y = y + jnp.einsum('bign,bgrpn->bigrp', cc, state) * jnp.exp(cum)[..., None]
        xw = xc * (jnp.exp(cum[:, -1:] - cum) * dtc)[..., None]
        new_state = (state * jnp.exp(cum[:, -1])[..., None, None]
                     + jnp.einsum('bjgrp,bjgn->bgrpn', xw, bc))
        return new_state, y

    state0 = ssm_state.astype(F32).reshape(bsz, SSD_GROUPS, SSD_HPG, SSD_HEAD_DIM, SSD_STATE)
    final, ys = lax.scan(step, state0, (_to_chunks(xs, nc), _to_chunks(bm, nc),
                                        _to_chunks(cm, nc), _to_chunks(dt, nc)))
    y = _from_chunks(ys) + d_skip.astype(F32).reshape(SSD_GROUPS, SSD_HPG)[..., None] * xs
    y = y.reshape(bsz, L, SSD_INNER) * jax.nn.silu(z)
    yg = y.reshape(bsz, L, SSD_GROUPS, SSD_INNER // SSD_GROUPS)
    yg = yg * lax.rsqrt(jnp.mean(yg * yg, axis=-1, keepdims=True) + EPS)
    y = yg.reshape(bsz, L, SSD_INNER) * norm_g.astype(F32)
    out = y.astype(u.dtype) @ w_out
    return out, final.reshape(bsz, SSD_HEADS, SSD_HEAD_DIM, SSD_STATE), new_conv


def _gla_mixer(u, s0, w_in, w_a2, b_a, norm_g, w_out):
    bsz, L, _ = u.shape
    proj = u @ w_in
    q = proj[..., :GLA_KEY].astype(F32).reshape(bsz, L, GLA_HEADS, GLA_DK) * (GLA_DK ** -0.5)
    k = proj[..., GLA_KEY:2 * GLA_KEY].astype(F32).reshape(bsz, L, GLA_HEADS, GLA_DK)
    v = proj[..., 2 * GLA_KEY:2 * GLA_KEY + GLA_VAL].astype(F32).reshape(bsz, L, GLA_HEADS, GLA_DV)
    r = proj[..., 2 * GLA_KEY + GLA_VAL:2 * GLA_KEY + 2 * GLA_VAL].astype(F32)
    lr = proj[..., 2 * GLA_KEY + 2 * GLA_VAL:]
    g = jax.nn.log_sigmoid((lr @ w_a2).astype(F32) + b_a.astype(F32)) / GLA_TAU
    g = g.reshape(bsz, L, GLA_HEADS, GLA_DK)
    cq = math.gcd(L, GLA_CHUNK)
    nc = L // cq
    mask = _causal_mask(cq)[None, :, :, None, None]

    def step(S, inp):
        qc, kc, vc, gc = inp
        cum = jnp.cumsum(gc, axis=1)
        o = jnp.einsum('bihd,bhdv->bihv', qc * jnp.exp(cum), S)
        seg = cum[:, :, None] - cum[:, None, :]
        decay = jnp.exp(jnp.where(mask, seg, -jnp.inf))
        att = jnp.einsum('bihd,bijhd->bijh', qc, decay * kc[:, None])
        o = o + jnp.einsum('bijh,bjhv->bihv', att, vc)
        kd = kc * jnp.exp(cum[:, -1:] - cum)
        S = S * jnp.exp(cum[:, -1])[..., None] + jnp.einsum('bjhd,bjhv->bhdv', kd, vc)
        return S, o

    final, outs = lax.scan(step, s0.astype(F32), (_to_chunks(q, nc), _to_chunks(k, nc),
                                                 _to_chunks(v, nc), _to_chunks(g, nc)))
    o = _from_chunks(outs)
    o = o * lax.rsqrt(jnp.mean(o * o, axis=-1, keepdims=True) + EPS) * norm_g.astype(F32)
    o = o.reshape(bsz, L, GLA_VAL) * jax.nn.silu(r)
    out = o.astype(u.dtype) @ w_out
    return out, final


def _mem_proj(mem, g_mem, w):
    m = _rmsnorm(mem, g_mem)
    return (m @ w).reshape(mem.shape[0], mem.shape[1], XA_HEADS, XA_HEAD_DIM)


def _cross_attn(h, mk, mv, w_q, w_o):
    bsz, L, _ = h.shape
    q = (h @ w_q).astype(F32).reshape(bsz, L, XA_HEADS, XA_HEAD_DIM)
    s = jnp.einsum('blhd,bmhd->bhlm', q, mk.astype(F32)) * (XA_HEAD_DIM ** -0.5)
    p = jax.nn.softmax(s, axis=-1)
    o = jnp.einsum('bhlm,bmhd->blhd', p, mv.astype(F32)).reshape(bsz, L, XA_DIM)
    return o.astype(h.dtype) @ w_o


def _sq_relu_mlp(h, w_up, w_down):
    return jnp.square(jax.nn.relu(h @ w_up)) @ w_down


def setup_inputs(seed: int = 0) -> dict:
    key = jax.random.key(seed)
    ks = iter(jax.random.split(key, 64))

    def nrm(shape, scale):
        return scale * jax.random.normal(next(ks), shape, F32)

    def gain(shape):
        return 1.0 + 0.02 * jax.random.normal(next(ks), shape, F32)

    def unif(shape, lo, hi):
        return jax.random.uniform(next(ks), shape, F32, minval=lo, maxval=hi)

    x_prompt = nrm((BATCH, SEQ, D_MODEL), 1.0)
    x_sample = nrm((DEC_BATCH, DEC_SEQ, D_MODEL), 1.0)
    state_s5_re = nrm((N_LAYERS_A, DEC_BATCH, S5_GROUPS, S5_STATE), 0.1)
    state_s5_im = nrm((N_LAYERS_A, DEC_BATCH, S5_GROUPS, S5_STATE), 0.1)
    state_ssd = nrm((N_LAYERS_B, DEC_BATCH, SSD_HEADS, SSD_HEAD_DIM, SSD_STATE), 0.1)
    state_ssd_conv = nrm((N_LAYERS_B, DEC_BATCH, SSD_CONV - 1, SSD_CONV_DIM), 1.0)
    state_gla = nrm((N_LAYERS_C, DEC_BATCH, GLA_HEADS, GLA_DK, GLA_DV), 0.5)
    cache_mem_k = nrm((DEPTH, DEC_BATCH, MEM_LEN, XA_HEADS, XA_HEAD_DIM), 1.0)
    cache_mem_v = nrm((DEPTH, DEC_BATCH, MEM_LEN, XA_HEADS, XA_HEAD_DIM), 1.0)
    mem_prompt = nrm((BATCH, MEM_LEN, D_MODEL), 1.0)

    norm_mix = gain((DEPTH, D_MODEL))
    norm_xattn = gain((DEPTH, D_MODEL))
    norm_mem = gain((DEPTH, D_MODEL))
    norm_mlp = gain((DEPTH, D_MODEL))
    norm_final = gain((D_MODEL,))
    xa_w_q = nrm((DEPTH, D_MODEL, XA_DIM), D_MODEL ** -0.5)
    xa_w_k = nrm((DEPTH, D_MODEL, XA_DIM), D_MODEL ** -0.5)
    xa_w_v = nrm((DEPTH, D_MODEL, XA_DIM), D_MODEL ** -0.5)
    xa_w_o = nrm((DEPTH, XA_DIM, D_MODEL), XA_DIM ** -0.5)
    mlp_w_up = nrm((DEPTH, D_MODEL, MLP_HIDDEN), D_MODEL ** -0.5)
    mlp_w_down = nrm((DEPTH, MLP_HIDDEN, D_MODEL), MLP_HIDDEN ** -0.5)

    s5_a_re = -0.5 + nrm((N_LAYERS_A, S5_GROUPS, S5_STATE), 0.01)
    s5_a_im = (jnp.pi * jnp.arange(S5_STATE, dtype=F32)[None, None, :]
               + nrm((N_LAYERS_A, S5_GROUPS, S5_STATE), 0.01))
    s5_log_dt = unif((N_LAYERS_A, S5_GROUPS), math.log(S5_DT_MIN), math.log(S5_DT_MAX))
    s5_b_re = nrm((N_LAYERS_A, S5_GROUPS, S5_STATE, S5_GROUP), (2 * S5_GROUP) ** -0.5)
    s5_b_im = nrm((N_LAYERS_A, S5_GROUPS, S5_STATE, S5_GROUP), (2 * S5_GROUP) ** -0.5)
    s5_c_re = nrm((N_LAYERS_A, S5_GROUPS, S5_GROUP, S5_STATE), S5_STATE ** -0.5)
    s5_c_im = nrm((N_LAYERS_A, S5_GROUPS, S5_GROUP, S5_STATE), S5_STATE ** -0.5)
    s5_d = nrm((N_LAYERS_A, D_MODEL), 1.0)
    s5_glu_w1 = nrm((N_LAYERS_A, D_MODEL, D_MODEL), D_MODEL ** -0.5)
    s5_glu_w2 = nrm((N_LAYERS_A, D_MODEL, D_MODEL), D_MODEL ** -0.5)

    ssd_w_in = nrm((N_LAYERS_B, D_MODEL, SSD_IN_DIM), D_MODEL ** -0.5)
    ssd_conv_w = nrm((N_LAYERS_B, SSD_CONV, SSD_CONV_DIM), SSD_CONV ** -0.5)
    ssd_conv_b = nrm((N_LAYERS_B, SSD_CONV_DIM), 0.01)
    dt0 = jnp.maximum(jnp.exp(unif((N_LAYERS_B, SSD_HEADS), math.log(SSD_DT_MIN), math.log(SSD_DT_MAX))), 1e-4)
    ssd_dt_bias = dt0 + jnp.log(-jnp.expm1(-dt0))
    ssd_a_log = jnp.log(unif((N_LAYERS_B, SSD_HEADS), 1.0, 16.0))
    ssd_d = gain((N_LAYERS_B, SSD_HEADS))
    ssd_norm = gain((N_LAYERS_B, SSD_INNER))
    ssd_w_out = nrm((N_LAYERS_B, SSD_INNER, D_MODEL), SSD_INNER ** -0.5)

    gla_w_in = nrm((N_LAYERS_C, D_MODEL, GLA_IN_DIM), D_MODEL ** -0.5)
    gla_w_a2 = nrm((N_LAYERS_C, GLA_RANK, GLA_KEY), GLA_RANK ** -0.5)
    gla_b_a = nrm((N_LAYERS_C, GLA_KEY), 0.1)
    gla_norm = gain((N_LAYERS_C, GLA_DV))
    gla_w_out = nrm((N_LAYERS_C, GLA_VAL, D_MODEL), GLA_VAL ** -0.5)

    return {
        'x_prompt': x_prompt, 'x_sample': x_sample,
        'state_s5_re': state_s5_re, 'state_s5_im': state_s5_im,
        'state_ssd': state_ssd, 'state_ssd_conv': state_ssd_conv, 'state_gla': state_gla,
        'cache_mem_k': cache_mem_k, 'cache_mem_v': cache_mem_v,
        'mem_prompt': mem_prompt,
        'norm_mix': norm_mix, 'norm_xattn': norm_xattn, 'norm_mem': norm_mem,
        'norm_mlp': norm_mlp, 'norm_final': norm_final,
        'xa_w_q': xa_w_q, 'xa_w_k': xa_w_k, 'xa_w_v': xa_w_v, 'xa_w_o': xa_w_o,
        'mlp_w_up': mlp_w_up, 'mlp_w_down': mlp_w_down,
        's5_a_re': s5_a_re, 's5_a_im': s5_a_im, 's5_log_dt': s5_log_dt,
        's5_b_re': s5_b_re, 's5_b_im': s5_b_im, 's5_c_re': s5_c_re, 's5_c_im': s5_c_im,
        's5_d': s5_d, 's5_glu_w1': s5_glu_w1, 's5_glu_w2': s5_glu_w2,
        'ssd_w_in': ssd_w_in, 'ssd_conv_w': ssd_conv_w, 'ssd_conv_b': ssd_conv_b,
        'ssd_dt_bias': ssd_dt_bias, 'ssd_a_log': ssd_a_log, 'ssd_d': ssd_d,
        'ssd_norm': ssd_norm, 'ssd_w_out': ssd_w_out,
        'gla_w_in': gla_w_in, 'gla_w_a2': gla_w_a2, 'gla_b_a': gla_b_a,
        'gla_norm': gla_norm, 'gla_w_out': gla_w_out,
    }


def reference(x_prompt, x_sample, state_s5_re, state_s5_im, state_ssd, state_ssd_conv, state_gla,
              cache_mem_k, cache_mem_v, mem_prompt,
              norm_mix, norm_xattn, norm_mem, norm_mlp, norm_final,
              xa_w_q, xa_w_k, xa_w_v, xa_w_o, mlp_w_up, mlp_w_down,
              s5_a_re, s5_a_im, s5_log_dt, s5_b_re, s5_b_im, s5_c_re, s5_c_im,
              s5_d, s5_glu_w1, s5_glu_w2,
              ssd_w_in, ssd_conv_w, ssd_conv_b, ssd_dt_bias, ssd_a_log, ssd_d, ssd_norm, ssd_w_out,
              gla_w_in, gla_w_a2, gla_b_a, gla_norm, gla_w_out):

    def run(x, mem_k, mem_v, st_a_re, st_a_im, st_b, st_bconv, st_c):
        new_a_re, new_a_im, new_b, new_bconv, new_c = [], [], [], [], []
        ia = ib = ic = 0
        for i in range(DEPTH):
            hn = _rmsnorm(x, norm_mix[i])
            kind = i % N_MIXERS
            if kind == 0:
                mix, hr, hi = _s5_mixer(hn, st_a_re[ia], st_a_im[ia], s5_a_re[ia], s5_a_im[ia],
                                        s5_log_dt[ia], s5_b_re[ia], s5_b_im[ia], s5_c_re[ia],
                                        s5_c_im[ia], s5_d[ia], s5_glu_w1[ia], s5_glu_w2[ia])
                new_a_re.append(hr)
                new_a_im.append(hi)
                ia += 1
            elif kind == 1:
                mix, sb, cb = _ssd_mixer(hn, st_b[ib], st_bconv[ib], ssd_w_in[ib], ssd_conv_w[ib],
                                         ssd_conv_b[ib], ssd_dt_bias[ib], ssd_a_log[ib], ssd_d[ib],
                                         ssd_norm[ib], ssd_w_out[ib])
                new_b.append(sb)
                new_bconv.append(cb)
                ib += 1
            else:
                mix, sc = _gla_mixer(hn, st_c[ic], gla_w_in[ic], gla_w_a2[ic], gla_b_a[ic],
                                     gla_norm[ic], gla_w_out[ic])
                new_c.append(sc)
                ic += 1
            x = x + mix
            x = x + _cross_attn(_rmsnorm(x, norm_xattn[i]), mem_k[i], mem_v[i], xa_w_q[i], xa_w_o[i])
            x = x + _sq_relu_mlp(_rmsnorm(x, norm_mlp[i]), mlp_w_up[i], mlp_w_down[i])
        return (_rmsnorm(x, norm_final), jnp.stack(new_a_re), jnp.stack(new_a_im),
                jnp.stack(new_b), jnp.stack(new_bconv), jnp.stack(new_c))

    pb = x_prompt.shape[0]
    p_mem_k = jnp.stack([_mem_proj(mem_prompt, norm_mem[i], xa_w_k[i]) for i in range(DEPTH)])
    p_mem_v = jnp.stack([_mem_proj(mem_prompt, norm_mem[i], xa_w_v[i]) for i in range(DEPTH)])
    y_prompt, p_s5_re, p_s5_im, p_ssd, p_ssd_conv, p_gla = run(
        x_prompt, p_mem_k, p_mem_v,
        jnp.zeros((N_LAYERS_A, pb, S5_GROUPS, S5_STATE), F32),
        jnp.zeros((N_LAYERS_A, pb, S5_GROUPS, S5_STATE), F32),
        jnp.zeros((N_LAYERS_B, pb, SSD_HEADS, SSD_HEAD_DIM, SSD_STATE), F32),
        jnp.zeros((N_LAYERS_B, pb, SSD_CONV - 1, SSD_CONV_DIM), F32),
        jnp.zeros((N_LAYERS_C, pb, GLA_HEADS, GLA_DK, GLA_DV), F32))

    y_sample, s_s5_re, s_s5_im, s_ssd, s_ssd_conv, s_gla = run(
        x_sample, cache_mem_k, cache_mem_v,
        state_s5_re, state_s5_im, state_ssd, state_ssd_conv, state_gla)

    return (y_prompt, y_sample, p_s5_re, p_s5_im, p_ssd, p_ssd_conv, p_gla, p_mem_k, p_mem_v,
            s_s5_re, s_s5_im, s_ssd, s_ssd_conv, s_gla)
```

```python
import functools
import math

import jax
import jax.numpy as jnp
from jax import lax
from jax.experimental import pallas as pl
from jax.experimental.pallas import tpu as pltpu

F32 = jnp.float32
BF16 = jnp.bfloat16
EPS = 1e-5

D_MODEL = 4096
DEPTH = 4
LANES = 128
ROWS = 128
VMEM_LIMIT = 56 * 1024 * 1024

S5_GROUP = 16
S5_GROUPS = D_MODEL // S5_GROUP
S5_STATE = 64
S5_GPB = LANES // S5_GROUP
S5_BLOCKS = D_MODEL // LANES
S5_SW = S5_GPB * S5_STATE

SSD_INNER = 2 * D_MODEL
SSD_HEAD_DIM = 64
SSD_HEADS = SSD_INNER // SSD_HEAD_DIM
SSD_GROUPS = 8
SSD_HPG = SSD_HEADS // SSD_GROUPS
SSD_STATE = 128
SSD_CONV = 4
SSD_GN = SSD_GROUPS * SSD_STATE
SSD_CONV_DIM = SSD_INNER + 2 * SSD_GN
SSD_IN_DIM = SSD_INNER + SSD_CONV_DIM + SSD_HEADS
SSD_GW = SSD_INNER // SSD_GROUPS

GLA_HEADS = 8
GLA_DK = D_MODEL // 2 // GLA_HEADS
GLA_DV = D_MODEL // GLA_HEADS
GLA_KEY = GLA_HEADS * GLA_DK
GLA_VAL = GLA_HEADS * GLA_DV
GLA_RANK = 16
GLA_TAU = 16.0
GLA_MAIN = 2 * GLA_KEY + 2 * GLA_VAL
GLA_SUB = 16

MEM_LEN = 256
XA_HEADS = 4
XA_HEAD_DIM = 128
XA_DIM = XA_HEADS * XA_HEAD_DIM
MLP_HIDDEN = 4 * D_MODEL


def _cparams(n_axes, vmem=VMEM_LIMIT):
    return pltpu.CompilerParams(dimension_semantics=("arbitrary",) * n_axes,
                                vmem_limit_bytes=vmem)


def _sigmoid(x):
    return 1.0 / (1.0 + jnp.exp(-x))


def _softplus(x):
    return jnp.maximum(x, 0.0) + jnp.log1p(jnp.exp(-jnp.abs(x)))


def _split3(x):
    hi = x.astype(BF16)
    r1 = x - hi.astype(F32)
    mid = r1.astype(BF16)
    lo = (r1 - mid.astype(F32)).astype(BF16)
    return hi, mid, lo


def _dot(a, b):
    return jnp.dot(a, b, preferred_element_type=F32)


def _dot_nt(a, b):
    return lax.dot_general(a, b, (((1,), (1,)), ((), ())), preferred_element_type=F32)


def _dot3_left(m01, x):
    return sum(_dot(m01, p) for p in _split3(x))


def _dot3_right(x, m01):
    return sum(_dot(p, m01) for p in _split3(x))


def _rmsnorm_kernel(x_ref, g_ref, o_ref):
    x = x_ref[...].astype(F32)
    ms = jnp.mean(x * x, axis=-1, keepdims=True)
    o_ref[...] = (x * lax.rsqrt(ms + EPS) * g_ref[...]).astype(o_ref.dtype)


def _rmsnorm(x, g_stack, layer, out_dtype, bm=256):
    m, d = x.shape
    return pl.pallas_call(
        _rmsnorm_kernel,
        grid=(m // bm,),
        in_specs=[pl.BlockSpec((bm, d), lambda i: (i, 0)),
                  pl.BlockSpec((None, 1, d), lambda i: (layer, 0, 0))],
        out_specs=pl.BlockSpec((bm, d), lambda i: (i, 0)),
        out_shape=jax.ShapeDtypeStruct((m, d), out_dtype),
        compiler_params=_cparams(1),
        name="rmsnorm",
    )(x, g_stack)


def _epilogue(mode, acc, acc2):
    if mode == "relu2":
        r = jnp.maximum(acc, 0.0)
        return r * r
    if mode == "glu":
        return acc * _sigmoid(acc2)
    return acc


def _mm_kernel_fullk(*refs, mode, has_res):
    it = iter(refs)
    x_ref = next(it)
    w_ref = next(it)
    w2_ref = next(it) if mode == "glu" else None
    res_ref = next(it) if has_res else None
    o_ref = next(it)
    wbf_ref = next(it)
    wbf2_ref = next(it) if mode == "glu" else None

    @pl.when(pl.program_id(1) == 0)
    def _():
        wbf_ref[...] = w_ref[...].astype(BF16)
        if mode == "glu":
            wbf2_ref[...] = w2_ref[...].astype(BF16)

    x = x_ref[...].astype(BF16)
    acc = _dot(x, wbf_ref[...])
    acc2 = _dot(x, wbf2_ref[...]) if mode == "glu" else None
    out = _epilogue(mode, acc, acc2)
    if has_res:
        out = out + res_ref[...]
    o_ref[...] = out.astype(o_ref.dtype)


def _mm_kernel_splitk(*refs, mode, has_res, nk):
    it = iter(refs)
    x_ref = next(it)
    w_ref = next(it)
    res_ref = next(it) if has_res else None
    o_ref = next(it)
    acc_ref = next(it)
    k = pl.program_id(2)

    @pl.when(k == 0)
    def _():
        acc_ref[...] = jnp.zeros_like(acc_ref)

    acc_ref[...] += _dot(x_ref[...].astype(BF16), w_ref[...].astype(BF16))

    @pl.when(k == nk - 1)
    def _():
        out = _epilogue(mode, acc_ref[...], None)
        if has_res:
            out = out + res_ref[...]
        o_ref[...] = out.astype(o_ref.dtype)


def _matmul(x, w, layer, *, n_out=None, w2=None, res=None, mode="plain", out_dtype=F32,
            bm=512, bn=512, bk=None):
    m, kdim = x.shape
    n = n_out if n_out is not None else w.shape[2]
    bk = kdim if bk is None else bk
    nk = kdim // bk
    has_res = res is not None
    assert m % bm == 0 and n % bn == 0 and kdim % bk == 0
    if nk == 1:
        grid = (n // bn, m // bm)
        in_specs = [pl.BlockSpec((bm, kdim), lambda j, i: (i, 0)),
                    pl.BlockSpec((None, kdim, bn), lambda j, i: (layer, 0, j))]
        args = [x, w]
        scratch = [pltpu.VMEM((kdim, bn), BF16)]
        if mode == "glu":
            in_specs.append(pl.BlockSpec((None, kdim, bn), lambda j, i: (layer, 0, j)))
            args.append(w2)
            scratch.append(pltpu.VMEM((kdim, bn), BF16))
        if has_res:
            in_specs.append(pl.BlockSpec((bm, bn), lambda j, i: (i, j)))
            args.append(res)
        out_spec = pl.BlockSpec((bm, bn), lambda j, i: (i, j))
        body = functools.partial(_mm_kernel_fullk, mode=mode, has_res=has_res)
        n_axes = 2
    else:
        assert mode != "glu"
        grid = (n // bn, m // bm, nk)
        in_specs = [pl.BlockSpec((bm, bk), lambda j, i, k: (i, k)),
                    pl.BlockSpec((None, bk, bn), lambda j, i, k: (layer, k, j))]
        args = [x, w]
        if has_res:
            in_specs.append(pl.BlockSpec((bm, bn), lambda j, i, k: (i, j)))
            args.append(res)
        out_spec = pl.BlockSpec((bm, bn), lambda j, i, k: (i, j))
        scratch = [pltpu.VMEM((bm, bn), F32)]
        body = functools.partial(_mm_kernel_splitk, mode=mode, has_res=has_res, nk=nk)
        n_axes = 3
    return pl.pallas_call(
        body, grid=grid, in_specs=in_specs, out_specs=out_spec,
        out_shape=jax.ShapeDtypeStruct((m, n), out_dtype),
        scratch_shapes=scratch, compiler_params=_cparams(n_axes),
        name="matmul_" + mode,
    )(*args)


def _xattn_kernel(q_ref, k_ref, v_ref, o_ref, *, bb):
    scale = XA_HEAD_DIM ** -0.5
    for ib in range(bb):
        for h in range(XA_HEADS):
            sl = slice(h * XA_HEAD_DIM, (h + 1) * XA_HEAD_DIM)
            q = q_ref[ib, :, sl].astype(BF16)
            k = k_ref[ib, :, sl].astype(BF16)
            v = v_ref[ib, :, sl].astype(BF16)
            s = _dot_nt(q, k) * scale
            p = jnp.exp(s - jnp.max(s, axis=-1, keepdims=True))
            l = jnp.sum(p, axis=-1, keepdims=True)
            o_ref[ib, :, sl] = (_dot(p.astype(BF16), v) / l).astype(o_ref.dtype)


def _xattn(q, mk, mv, layer, *, bb, lq):
    b, l, _ = q.shape
    return pl.pallas_call(
        functools.partial(_xattn_kernel, bb=bb),
        grid=(b // bb, l // lq),
        in_specs=[pl.BlockSpec((bb, lq, XA_DIM), lambda i, j: (i, j, 0)),
                  pl.BlockSpec((None, bb, MEM_LEN, XA_DIM), lambda i, j: (layer, i, 0, 0)),
                  pl.BlockSpec((None, bb, MEM_LEN, XA_DIM), lambda i, j: (layer, i, 0, 0))],
        out_specs=pl.BlockSpec((bb, lq, XA_DIM), lambda i, j: (i, j, 0)),
        out_shape=jax.ShapeDtypeStruct((b, l, XA_DIM), F32),
        compiler_params=_cparams(2),
        name="xattn",
    )(q, mk, mv)


def _s5_kernel(u_ref, h0r_ref, h0i_ref, ktr_ref, bp_ref, cp_ref, at_ref, d_ref,
               act_ref, hr_ref, hi_ref, ucat_ref, s_ref, hin_ref, ybuf_ref, *, nb, nc, t_len):
    r = nb * nc
    for s in range(t_len):
        ucat_ref[:, s * LANES:(s + 1) * LANES] = u_ref[pl.ds(s, r, stride=t_len), :].astype(BF16)
    half = S5_SW // LANES
    s_all = _dot(ucat_ref[...], bp_ref[0])
    for k in range(2 * half):
        s_ref[k] = s_all[:, k * LANES:(k + 1) * LANES]
    a_sl = [at_ref[0, :, k * LANES:(k + 1) * LANES] for k in range(2 * half)]

    def body(c, carry):
        rows_c = pl.ds(c, nb, stride=nc)
        new = []
        for k in range(half):
            hr, hi = carry[k], carry[half + k]
            hin_ref[k, rows_c, :] = hr
            hin_ref[half + k, rows_c, :] = hi
            new.append((a_sl[k] * hr - a_sl[half + k] * hi + s_ref[k, rows_c, :],
                        a_sl[k] * hi + a_sl[half + k] * hr + s_ref[half + k, rows_c, :]))
        return tuple(n[0] for n in new) + tuple(n[1] for n in new)

    h0 = (tuple(h0r_ref[:, k * LANES:(k + 1) * LANES] for k in range(half))
          + tuple(h0i_ref[:, k * LANES:(k + 1) * LANES] for k in range(half)))
    hfin = lax.fori_loop(0, nc, body, h0)
    for k in range(half):
        hr_ref[:, k * LANES:(k + 1) * LANES] = hfin[k]
        hi_ref[:, k * LANES:(k + 1) * LANES] = hfin[half + k]
    hin = jnp.concatenate([hin_ref[k] for k in range(2 * half)], axis=1).astype(BF16)
    for t in range(t_len):
        y = _dot(hin, cp_ref[0, t])
        y = y + _dot(ucat_ref[:, :(t + 1) * LANES], ktr_ref[0, (t_len - 1 - t) * LANES:, :])
        ut = u_ref[pl.ds(t, r, stride=t_len), :]
        y = y + d_ref[...] * ut
        ybuf_ref[pl.ds(t, r, stride=t_len), :] = jax.nn.gelu(y)
    act_ref[...] = ybuf_ref[...].astype(act_ref.dtype)


def _s5_prep(a_re, a_im, log_dt, b_re, b_im, c_re, c_im, t_len):
    hp = lax.Precision.HIGHEST
    g, p = a_re.shape
    lam_re = a_re.astype(F32)
    lam_im = a_im.astype(F32)
    dt = jnp.exp(log_dt.astype(F32))[:, None]
    kk = jnp.arange(t_len + 1, dtype=F32)[:, None, None]
    magk = jnp.exp(kk * (lam_re * dt))
    angk = kk * (lam_im * dt)
    pw_re = magk * jnp.cos(angk)
    pw_im = magk * jnp.sin(angk)
    ab_re, ab_im = pw_re[1], pw_im[1]
    den = lam_re * lam_re + lam_im * lam_im
    f_re = ((ab_re - 1.0) * lam_re + ab_im * lam_im) / den
    f_im = (ab_im * lam_re - (ab_re - 1.0) * lam_im) / den
    br = b_re.astype(F32)
    bi = b_im.astype(F32)
    bb_re = f_re[..., None] * br - f_im[..., None] * bi
    bb_im = f_re[..., None] * bi + f_im[..., None] * br
    cr = c_re.astype(F32)
    ci = c_im.astype(F32)
    eye = jnp.eye(S5_GPB, dtype=F32)
    nblk = g // S5_GPB

    kr = jnp.float32(t_len - 1) - jnp.arange(t_len, dtype=F32)[:, None, None]
    magr = jnp.exp(kr * (lam_re * dt))
    angr = kr * (lam_im * dt)
    rev = magr * jnp.cos(angr), magr * jnp.sin(angr)
    v_re = rev[0][..., None] * bb_re - rev[1][..., None] * bb_im
    v_im = rev[0][..., None] * bb_im + rev[1][..., None] * bb_re
    vc = jnp.stack([v_re, v_im], axis=0)
    vc = vc.reshape(2, t_len, nblk, S5_GPB, p, S5_GROUP)
    bp = jnp.einsum('qsjgpc,gh->jsgcqhp', vc, eye)
    bp = bp.reshape(nblk, t_len * LANES, 2 * S5_SW)

    w_re = cr[None] * pw_re[1:, :, None, :] - ci[None] * pw_im[1:, :, None, :]
    w_im = cr[None] * pw_im[1:, :, None, :] + ci[None] * pw_re[1:, :, None, :]
    wc = jnp.stack([w_re, -w_im], axis=0).reshape(2, t_len, nblk, S5_GPB, S5_GROUP, p)
    cp = jnp.einsum('qtjgcp,gh->jtqhpgc', wc, eye)
    cp = cp.reshape(nblk, t_len, 2 * S5_SW, LANES)

    kt = (jnp.einsum('gcp,tgpd->tgcd', cr, v_re, precision=hp)
          - jnp.einsum('gcp,tgpd->tgcd', ci, v_im, precision=hp))
    kt = kt.reshape(t_len, nblk, S5_GPB, S5_GROUP, S5_GROUP)
    ktr = jnp.einsum('mjgcd,gh->jmgdhc', kt, eye).reshape(nblk, t_len * LANES, LANES)

    at = jnp.concatenate([pw_re[t_len].reshape(nblk, 1, S5_SW),
                          pw_im[t_len].reshape(nblk, 1, S5_SW)], axis=2)
    return ktr.astype(BF16), bp.astype(BF16), cp.astype(BF16), at


def _s5_core(u, row0, nb, seq, t_len, h0_re, h0_im, ops, d_skip):
    ktr, bp, cp, at = ops
    nc = seq // t_len
    r = nb * nc
    nrows = nb * seq
    assert row0 % nrows == 0
    rb = row0 // nrows
    h0r = h0_re.reshape(nb, S5_GROUPS * S5_STATE)
    h0i = h0_im.reshape(nb, S5_GROUPS * S5_STATE)
    out_shapes = (jax.ShapeDtypeStruct((nrows, D_MODEL), BF16),
                  jax.ShapeDtypeStruct((nb, S5_GROUPS * S5_STATE), F32),
                  jax.ShapeDtypeStruct((nb, S5_GROUPS * S5_STATE), F32))
    act, hr, hi = pl.pallas_call(
        functools.partial(_s5_kernel, nb=nb, nc=nc, t_len=t_len),
        grid=(S5_BLOCKS,),
        in_specs=[pl.BlockSpec((nrows, LANES), lambda j: (rb, j)),
                  pl.BlockSpec((nb, S5_SW), lambda j: (0, j)),
                  pl.BlockSpec((nb, S5_SW), lambda j: (0, j)),
                  pl.BlockSpec((1, t_len * LANES, LANES), lambda j: (j, 0, 0)),
                  pl.BlockSpec((1, t_len * LANES, 2 * S5_SW), lambda j: (j, 0, 0)),
                  pl.BlockSpec((1, t_len, 2 * S5_SW, LANES), lambda j: (j, 0, 0, 0)),
                  pl.BlockSpec((1, 1, 2 * S5_SW), lambda j: (j, 0, 0)),
                  pl.BlockSpec((1, LANES), lambda j: (0, j))],
        out_specs=(pl.BlockSpec((nrows, LANES), lambda j: (0, j)),
                   pl.BlockSpec((nb, S5_SW), lambda j: (0, j)),
                   pl.BlockSpec((nb, S5_SW), lambda j: (0, j))),
        out_shape=out_shapes,
        scratch_shapes=[pltpu.VMEM((r, t_len * LANES), BF16),
                        pltpu.VMEM((2 * S5_SW // LANES, r, LANES), F32),
                        pltpu.VMEM((2 * S5_SW // LANES, r, LANES), F32),
                        pltpu.VMEM((nrows, LANES), F32)],
        compiler_params=_cparams(1),
        name="s5_core",
    )(u, h0r, h0i, ktr, bp, cp, at, d_skip.reshape(1, D_MODEL))
    return (act, hr.reshape(nb, S5_GROUPS, S5_STATE), hi.reshape(nb, S5_GROUPS, S5_STATE))


def _ssd_kernel(z_ref, x_ref, b_ref, c_ref, dt_ref, dtt_ref,
                cwx_ref, cwb_ref, cwc_ref, cbx_ref, cbb_ref, cbc_ref,
                csx_ref, csb_ref, csc_ref,
                dtb_ref, dtbt_ref, alog_ref, alogt_ref, dexp_ref, norm_ref, st0_ref,
                y_ref, st_ref, xbuf, bbuf, cbuf, *, nb, qt):
    rows = nb * qt
    c = pl.program_id(2)
    pad = 8
    hist = SSD_CONV - 1

    @pl.when(c == 0)
    def _():
        st_ref[...] = st0_ref[...]
        xbuf[:, pad - hist:pad, :] = csx_ref[...]
        bbuf[:, pad - hist:pad, :] = csb_ref[...]
        cbuf[:, pad - hist:pad, :] = csc_ref[...]

    def conv(buf, raw_ref, w_ref, bias_ref, width):
        buf[:, pad:, :] = raw_ref[...].reshape(nb, qt, width)
        acc = bias_ref[...].reshape(1, 1, width)
        for k in range(SSD_CONV):
            acc = acc + buf[:, pad - hist + k:pad - hist + k + qt, :] * w_ref[k:k + 1, :].reshape(1, 1, width)
        tail = buf[:, pad + qt - hist:pad + qt, :]
        buf[:, pad - hist:pad, :] = tail
        acc = acc * _sigmoid(acc)
        return acc.reshape(rows, width)

    xs = conv(xbuf, x_ref, cwx_ref, cbx_ref, SSD_GW)
    bs = conv(bbuf, b_ref, cwb_ref, cbb_ref, SSD_STATE).astype(BF16)
    cs = conv(cbuf, c_ref, cwc_ref, cbc_ref, SSD_STATE).astype(BF16)

    dt = _softplus(dt_ref[0] + dtb_ref[0])
    dtt = _softplus(dtt_ref[0] + dtbt_ref[0])
    da = dt * (-jnp.exp(alog_ref[0]))
    dat = dtt * (-jnp.exp(alogt_ref[0]))

    ri = lax.broadcasted_iota(jnp.int32, (rows, rows), 0)
    ci = lax.broadcasted_iota(jnp.int32, (rows, rows), 1)
    same = (ri // qt) == (ci // qt)
    tri = same & (ri >= ci)
    tri_bf = tri.astype(BF16)
    trit_bf = (same & (ri <= ci)).astype(BF16)
    same_bf = same.astype(BF16)
    cum = _dot3_left(tri_bf, da)
    cumt = _dot3_right(dat, trit_bf)
    tot = _dot3_left(same_bf, da)
    cb = _dot_nt(cs, bs)

    lane = lax.broadcasted_iota(jnp.int32, (1, LANES), 1)
    lo = lane < SSD_HEAD_DIM
    srow = lax.broadcasted_iota(jnp.int32, (LANES, 1), 0) < SSD_HEAD_DIM
    row_seq = lax.broadcasted_iota(jnp.int32, (rows, 1), 0) // qt
    neg_inf = jnp.float32(-jnp.inf)
    bs_seq = [bs if nb == 1 else jnp.where(row_seq == ib, bs, jnp.zeros_like(bs)) for ib in range(nb)]

    ys = []
    for p in range(SSD_HPG // 2):
        ra, rb = 2 * p, 2 * p + 1
        ws = []
        for r in (ra, rb):
            seg = cum[:, r:r + 1] - cumt[r:r + 1, :]
            dec = jnp.exp(jnp.where(tri, seg, neg_inf))
            ws.append((cb * dec * dtt[r:r + 1, :]).astype(BF16))
        xp = xs[:, p * LANES:(p + 1) * LANES]
        y = (_dot(ws[0], jnp.where(lo, xp, 0.0).astype(BF16))
             + _dot(ws[1], jnp.where(lo, 0.0, xp).astype(BF16)))
        ecol = jnp.where(lo, jnp.exp(cum[:, ra:ra + 1]), jnp.exp(cum[:, rb:rb + 1]))
        inter = []
        for ib in range(nb):
            st = st_ref[ib, 0, p * LANES:(p + 1) * LANES, :].astype(BF16)
            inter.append(_dot_nt(cs[ib * qt:(ib + 1) * qt, :], st))
        inter = inter[0] if nb == 1 else jnp.concatenate(inter, axis=0)
        y = y + inter * ecol + dexp_ref[:, p * LANES:(p + 1) * LANES] * xp
        ys.append(y)

        sc = jnp.where(lo, jnp.exp(tot[:, ra:ra + 1] - cum[:, ra:ra + 1]) * dt[:, ra:ra + 1],
                       jnp.exp(tot[:, rb:rb + 1] - cum[:, rb:rb + 1]) * dt[:, rb:rb + 1])
        xwt = jnp.transpose(xp * sc).astype(BF16)
        for ib in range(nb):
            upd = _dot(xwt, bs_seq[ib])
            r0 = ib * qt
            dcol = jnp.where(srow, jnp.exp(tot[r0:r0 + 1, ra:ra + 1]), jnp.exp(tot[r0:r0 + 1, rb:rb + 1]))
            old = st_ref[ib, 0, p * LANES:(p + 1) * LANES, :]
            st_ref[ib, 0, p * LANES:(p + 1) * LANES, :] = old * dcol + upd

    y = jnp.concatenate(ys, axis=1)
    zg = z_ref[...]
    y = y * (zg * _sigmoid(zg))
    ms = jnp.mean(y * y, axis=-1, keepdims=True)
    y_ref[...] = (y * lax.rsqrt(ms + EPS) * norm_ref[...]).astype(y_ref.dtype)


def _ssd_core(proj, dtp, dtt, row_blk0, nseq, nb, qt, nc, st0, conv0, params):
    cw, cbias, dtb, dtbt, alog, alogt, dexp, norm = params
    nbb = nseq // nb
    xoff = SSD_INNER // SSD_GW
    boff = (SSD_INNER + SSD_INNER) // SSD_STATE
    coff = boff + SSD_GN // SSD_STATE
    cwb_off = SSD_INNER // SSD_STATE
    cwc_off = cwb_off + SSD_GN // SSD_STATE

    def rblk(i, g, c):
        return row_blk0 + i * nc + c

    st0 = st0.reshape(nseq, SSD_GROUPS, SSD_GW, SSD_STATE)
    hist = SSD_CONV - 1
    in_specs = [
        pl.BlockSpec((ROWS, SSD_GW), lambda i, g, c: (rblk(i, g, c), g)),
        pl.BlockSpec((ROWS, SSD_GW), lambda i, g, c: (rblk(i, g, c), xoff + g)),
        pl.BlockSpec((ROWS, SSD_STATE), lambda i, g, c: (rblk(i, g, c), boff + g)),
        pl.BlockSpec((ROWS, SSD_STATE), lambda i, g, c: (rblk(i, g, c), coff + g)),
        pl.BlockSpec((1, ROWS, LANES), lambda i, g, c: (g, rblk(i, g, c), 0)),
        pl.BlockSpec((1, SSD_HPG, ROWS), lambda i, g, c: (g, 0, rblk(i, g, c))),
        pl.BlockSpec((SSD_CONV, SSD_GW), lambda i, g, c: (0, g)),
        pl.BlockSpec((SSD_CONV, SSD_STATE), lambda i, g, c: (0, cwb_off + g)),
        pl.BlockSpec((SSD_CONV, SSD_STATE), lambda i, g, c: (0, cwc_off + g)),
        pl.BlockSpec((1, SSD_GW), lambda i, g, c: (0, g)),
        pl.BlockSpec((1, SSD_STATE), lambda i, g, c: (0, cwb_off + g)),
        pl.BlockSpec((1, SSD_STATE), lambda i, g, c: (0, cwc_off + g)),
        pl.BlockSpec((nb, hist, SSD_GW), lambda i, g, c: (i, 0, g)),
        pl.BlockSpec((nb, hist, SSD_STATE), lambda i, g, c: (i, 0, cwb_off + g)),
        pl.BlockSpec((nb, hist, SSD_STATE), lambda i, g, c: (i, 0, cwc_off + g)),
        pl.BlockSpec((1, 1, LANES), lambda i, g, c: (g, 0, 0)),
        pl.BlockSpec((1, SSD_HPG, 1), lambda i, g, c: (g, 0, 0)),
        pl.BlockSpec((1, 1, LANES), lambda i, g, c: (g, 0, 0)),
        pl.BlockSpec((1, SSD_HPG, 1), lambda i, g, c: (g, 0, 0)),
        pl.BlockSpec((1, SSD_GW), lambda i, g, c: (0, g)),
        pl.BlockSpec((1, SSD_GW), lambda i, g, c: (0, g)),
        pl.BlockSpec((nb, 1, SSD_GW, SSD_STATE), lambda i, g, c: (i, g, 0, 0)),
    ]
    out_specs = (pl.BlockSpec((ROWS, SSD_GW), lambda i, g, c: (i * nc + c, g)),
                 pl.BlockSpec((nb, 1, SSD_GW, SSD_STATE), lambda i, g, c: (i, g, 0, 0)))
    out_shape = (jax.ShapeDtypeStruct((nseq * qt * nc, SSD_INNER), BF16),
                 jax.ShapeDtypeStruct((nseq, SSD_GROUPS, SSD_GW, SSD_STATE), F32))
    y, st = pl.pallas_call(
        functools.partial(_ssd_kernel, nb=nb, qt=qt),
        grid=(nbb, SSD_GROUPS, nc),
        in_specs=in_specs, out_specs=out_specs, out_shape=out_shape,
        scratch_shapes=[pltpu.VMEM((nb, 8 + qt, SSD_GW), F32),
                        pltpu.VMEM((nb, 8 + qt, SSD_STATE), F32),
                        pltpu.VMEM((nb, 8 + qt, SSD_STATE), F32)],
        compiler_params=_cparams(3),
        name="ssd_core",
    )(proj, proj, proj, proj, dtp, dtt, cw, cw, cw, cbias, cbias, cbias,
      conv0, conv0, conv0, dtb, dtbt, alog, alogt, dexp, norm, st0)
    return y, st.reshape(nseq, SSD_HEADS, SSD_HEAD_DIM, SSD_STATE)


def _gla_kernel(q_ref, k_ref, v_ref, r_ref, lr_ref, wa2_ref, ba_ref, norm_ref, s0_ref,
                o_ref, s_ref, cum_s, q_s, k_s, att_s, *, nb, qt, wq):
    rows = nb * qt
    c = pl.program_id(2)

    @pl.when(c == 0)
    def _():
        s_ref[...] = s0_ref[...]

    qv = q_ref[...] * (GLA_DK ** -0.5)
    kv = k_ref[...]
    v_bf = v_ref[...].astype(BF16)
    gate = _dot(lr_ref[...].astype(BF16), wa2_ref[...].astype(BF16)) + ba_ref[...]
    g = -_softplus(-gate) / GLA_TAU

    ri = lax.broadcasted_iota(jnp.int32, (rows, rows), 0)
    ci = lax.broadcasted_iota(jnp.int32, (rows, rows), 1)
    same = (ri // qt) == (ci // qt)
    tri_bf = (same & (ri >= ci)).astype(BF16)
    same_bf = same.astype(BF16)
    cum = _dot3_left(tri_bf, g)
    tot = _dot3_left(same_bf, g)
    cum_s[...] = cum
    q_s[...] = qv
    k_s[...] = kv

    qe = (qv * jnp.exp(cum)).astype(BF16)
    o_parts = [_dot(qe[ib * qt:(ib + 1) * qt, :], s_ref[ib, 0].astype(BF16)) for ib in range(nb)]
    o = o_parts[0] if nb == 1 else jnp.concatenate(o_parts, axis=0)

    neg_inf = jnp.float32(-jnp.inf)
    att_s[...] = jnp.zeros_like(att_s)
    if qt > wq:
        assert nb == 1
        rowi = lax.broadcasted_iota(jnp.int32, (rows, 1), 0)
        for blk in range(1, qt // wq):
            lo_r, hi_r = blk * wq, (blk + 1) * wq
            c0 = cum[lo_r - 1:lo_r, :]
            q_blk = (qv[lo_r:hi_r, :] * jnp.exp(cum[lo_r:hi_r, :] - c0)).astype(BF16)
            k_all = (kv * jnp.exp(jnp.where(rowi < lo_r, c0 - cum, neg_inf))).astype(BF16)
            att_s[lo_r:hi_r, :] = _dot_nt(q_blk, k_all)

    lane = lax.broadcasted_iota(jnp.int32, (1, rows), 1)
    wrow = lax.broadcasted_iota(jnp.int32, (wq, 1), 0)

    def body(j, carry):
        w0 = pl.multiple_of((j // wq) * wq, wq)
        cw = cum_s[pl.ds(w0, wq), :]
        qw = q_s[pl.ds(w0, wq), :]
        cj = cum_s[pl.ds(j, 1), :]
        kj = k_s[pl.ds(j, 1), :]
        e = jnp.exp(jnp.where(wrow + w0 >= j, cw - cj, neg_inf))
        col = jnp.sum(qw * e * kj, axis=-1, keepdims=True)
        cur = att_s[pl.ds(w0, wq), :]
        att_s[pl.ds(w0, wq), :] = jnp.where(lane == j, col, cur)
        return carry

    lax.fori_loop(0, rows, body, 0)
    o = o + _dot(att_s[...].astype(BF16), v_bf)

    kdt = jnp.transpose(kv * jnp.exp(tot - cum)).astype(BF16)
    gt = jnp.transpose(g)
    rsel = lax.broadcasted_iota(jnp.int32, (rows, LANES), 0) // qt
    row_seq = lax.broadcasted_iota(jnp.int32, (rows, 1), 0) // qt
    for ib in range(nb):
        sel = (rsel == ib).astype(BF16)
        dec = jnp.exp(_dot3_right(gt, sel))
        dec = jnp.concatenate([dec] * (GLA_DV // LANES), axis=1)
        v_ib = v_bf if nb == 1 else jnp.where(row_seq == ib, v_bf, jnp.zeros_like(v_bf))
        s_ref[ib, 0] = s_ref[ib, 0] * dec + _dot(kdt, v_ib)

    ms = jnp.mean(o * o, axis=-1, keepdims=True)
    o = o * lax.rsqrt(ms + EPS) * norm_ref[...]
    rr = r_ref[...]
    o_ref[...] = (o * (rr * _sigmoid(rr))).astype(o_ref.dtype)


def _gla_core(proj, lrp, row_blk0, nseq, nb, qt, nc, wq, s0, params):
    wa2p, ba, norm = params
    nbb = nseq // nb
    koff = GLA_KEY // GLA_DK
    voff = 2 * GLA_KEY // GLA_DV
    roff = voff + GLA_VAL // GLA_DV

    def rblk(i, h, c):
        return row_blk0 + i * nc + c

    in_specs = [
        pl.BlockSpec((ROWS, GLA_DK), lambda i, h, c: (rblk(i, h, c), h)),
        pl.BlockSpec((ROWS, GLA_DK), lambda i, h, c: (rblk(i, h, c), koff + h)),
        pl.BlockSpec((ROWS, GLA_DV), lambda i, h, c: (rblk(i, h, c), voff + h)),
        pl.BlockSpec((ROWS, GLA_DV), lambda i, h, c: (rblk(i, h, c), roff + h)),
        pl.BlockSpec((ROWS, LANES), lambda i, h, c: (rblk(i, h, c), 0)),
        pl.BlockSpec((LANES, GLA_DK), lambda i, h, c: (0, h)),
        pl.BlockSpec((1, GLA_DK), lambda i, h, c: (0, h)),
        pl.BlockSpec((1, GLA_DV), lambda i, h, c: (0, 0)),
        pl.BlockSpec((nb, 1, GLA_DK, GLA_DV), lambda i, h, c: (i, h, 0, 0)),
    ]
    out_specs = (pl.BlockSpec((ROWS, GLA_DV), lambda i, h, c: (i * nc + c, h)),
                 pl.BlockSpec((nb, 1, GLA_DK, GLA_DV), lambda i, h, c: (i, h, 0, 0)))
    out_shape = (jax.ShapeDtypeStruct((nseq * qt * nc, GLA_VAL), BF16),
                 jax.ShapeDtypeStruct((nseq, GLA_HEADS, GLA_DK, GLA_DV), F32))
    return pl.pallas_call(
        functools.partial(_gla_kernel, nb=nb, qt=qt, wq=wq),
        grid=(nbb, GLA_HEADS, nc),
        in_specs=in_specs, out_specs=out_specs, out_shape=out_shape,
        scratch_shapes=[pltpu.VMEM((ROWS, GLA_DK), F32), pltpu.VMEM((ROWS, GLA_DK), F32),
                        pltpu.VMEM((ROWS, GLA_DK), F32), pltpu.VMEM((ROWS, ROWS), F32)],
        compiler_params=_cparams(3),
        name="gla_core",
    )(proj, proj, proj, proj, lrp, wa2p, ba, norm, s0)


def kernel(x_prompt, x_sample, state_s5_re, state_s5_im, state_ssd, state_ssd_conv, state_gla, cache_mem_k, cache_mem_v, mem_prompt, norm_mix, norm_xattn, norm_mem, norm_mlp, norm_final, xa_w_q, xa_w_k, xa_w_v, xa_w_o, mlp_w_up, mlp_w_down, s5_a_re, s5_a_im, s5_log_dt, s5_b_re, s5_b_im, s5_c_re, s5_c_im, s5_d, s5_glu_w1, s5_glu_w2, ssd_w_in, ssd_conv_w, ssd_conv_b, ssd_dt_bias, ssd_a_log, ssd_d, ssd_norm, ssd_w_out, gla_w_in, gla_w_a2, gla_b_a, gla_norm, gla_w_out):
    pb, pseq, d = x_prompt.shape
    sb, sseq, _ = x_sample.shape
    mp = pb * pseq
    ms = sb * sseq
    assert mp % ROWS == 0 and ms % ROWS == 0 and pseq % ROWS == 0 and ROWS % sseq == 0
    x = jnp.concatenate([x_prompt.reshape(mp, d), x_sample.reshape(ms, d)], axis=0)
    m = mp + ms
    p_blk0 = 0
    s_blk0 = mp // ROWS
    s_nb = ROWS // sseq

    g_mix = norm_mix.reshape(DEPTH, 1, d)
    g_xa = norm_xattn.reshape(DEPTH, 1, d)
    g_mem = norm_mem.reshape(DEPTH, 1, d)
    g_mlp = norm_mlp.reshape(DEPTH, 1, d)

    mem2 = mem_prompt.reshape(pb * MEM_LEN, d)
    pk, pv = [], []
    for i in range(DEPTH):
        mn = _rmsnorm(mem2, g_mem, i, BF16)
        pk.append(_matmul(mn, xa_w_k, i, bm=512, bn=XA_DIM))
        pv.append(_matmul(mn, xa_w_v, i, bm=512, bn=XA_DIM))
    p_mem_k = jnp.stack(pk).reshape(DEPTH, pb, MEM_LEN, XA_DIM)
    p_mem_v = jnp.stack(pv).reshape(DEPTH, pb, MEM_LEN, XA_DIM)
    c_mem_k = cache_mem_k.reshape(DEPTH, sb, MEM_LEN, XA_DIM)
    c_mem_v = cache_mem_v.reshape(DEPTH, sb, MEM_LEN, XA_DIM)

    outs = {k: [] for k in ("pa_re", "pa_im", "pb", "pbc", "pc", "sa_re", "sa_im", "sb", "sbc", "sc")}
    ia = ib = ic = 0
    for i in range(DEPTH):
        kind = i % 3
        if kind == 0:
            hn = _rmsnorm(x, g_mix, i, F32)
            ops_p = _s5_prep(s5_a_re[ia], s5_a_im[ia], s5_log_dt[ia], s5_b_re[ia], s5_b_im[ia],
                             s5_c_re[ia], s5_c_im[ia], 16)
            ops_s = _s5_prep(s5_a_re[ia], s5_a_im[ia], s5_log_dt[ia], s5_b_re[ia], s5_b_im[ia],
                             s5_c_re[ia], s5_c_im[ia], sseq)
            zero = jnp.zeros((pb, S5_GROUPS, S5_STATE), F32)
            act_p, hr_p, hi_p = _s5_core(hn, 0, pb, pseq, 16, zero, zero, ops_p, s5_d[ia])
            act_s, hr_s, hi_s = _s5_core(hn, mp, sb, sseq, sseq, state_s5_re[ia], state_s5_im[ia],
                                         ops_s, s5_d[ia])
            outs["pa_re"].append(hr_p)
            outs["pa_im"].append(hi_p)
            outs["sa_re"].append(hr_s)
            outs["sa_im"].append(hi_s)
            act = jnp.concatenate([act_p, act_s], axis=0)
            x = _matmul(act, s5_glu_w1, ia, w2=s5_glu_w2, res=x, mode="glu", bn=256)
            ia += 1
        elif kind == 1:
            hn = _rmsnorm(x, g_mix, i, BF16)
            proj = _matmul(hn, ssd_w_in, ib, bn=640)
            dt_raw = proj[:, SSD_INNER + SSD_CONV_DIM:].reshape(m, SSD_GROUPS, SSD_HPG)
            dtt = jnp.transpose(dt_raw, (1, 2, 0))
            dtp = jnp.pad(jnp.transpose(dt_raw, (1, 0, 2)), ((0, 0), (0, 0), (0, LANES - SSD_HPG)))
            pad_h = ((0, 0), (0, 0), (0, LANES - SSD_HPG))
            params = (ssd_conv_w[ib], ssd_conv_b[ib].reshape(1, SSD_CONV_DIM),
                      jnp.pad(ssd_dt_bias[ib].reshape(SSD_GROUPS, 1, SSD_HPG), pad_h),
                      ssd_dt_bias[ib].reshape(SSD_GROUPS, SSD_HPG, 1),
                      jnp.pad(ssd_a_log[ib].reshape(SSD_GROUPS, 1, SSD_HPG), pad_h),
                      ssd_a_log[ib].reshape(SSD_GROUPS, SSD_HPG, 1),
                      jnp.repeat(ssd_d[ib], SSD_HEAD_DIM).reshape(1, SSD_INNER),
                      ssd_norm[ib].reshape(1, SSD_INNER))
            y_p, st_p = _ssd_core(proj, dtp, dtt, p_blk0, pb, 1, ROWS, pseq // ROWS,
                                  jnp.zeros((pb, SSD_HEADS, SSD_HEAD_DIM, SSD_STATE), F32),
                                  jnp.zeros((pb, SSD_CONV - 1, SSD_CONV_DIM), F32), params)
            y_s, st_s = _ssd_core(proj, dtp, dtt, s_blk0, sb, s_nb, sseq, 1,
                                  state_ssd[ib], state_ssd_conv[ib], params)
            xbc = proj[:, SSD_INNER:SSD_INNER + SSD_CONV_DIM]
            outs["pb"].append(st_p)
            outs["sb"].append(st_s)
            outs["pbc"].append(xbc[:mp].reshape(pb, pseq, SSD_CONV_DIM)[:, pseq - (SSD_CONV - 1):])
            outs["sbc"].append(xbc[mp:].reshape(sb, sseq, SSD_CONV_DIM)[:, sseq - (SSD_CONV - 1):])
            yb = jnp.concatenate([y_p, y_s], axis=0)
            x = _matmul(yb, ssd_w_out, ib, res=x, bm=1152, bn=1024, bk=512)
            ib += 1
        else:
            hn = _rmsnorm(x, g_mix, i, BF16)
            proj = _matmul(hn, gla_w_in, ic, n_out=GLA_MAIN)
            w_lr = jnp.pad(gla_w_in[ic][:, GLA_MAIN:], ((0, 0), (0, LANES - GLA_RANK)))[None]
            lrp = _matmul(hn, w_lr, 0, bn=LANES)
            params = (jnp.pad(gla_w_a2[ic], ((0, LANES - GLA_RANK), (0, 0))),
                      gla_b_a[ic].reshape(1, GLA_KEY), gla_norm[ic].reshape(1, GLA_DV))
            o_p, s_p = _gla_core(proj, lrp, p_blk0, pb, 1, ROWS, pseq // ROWS, GLA_SUB,
                                 jnp.zeros((pb, GLA_HEADS, GLA_DK, GLA_DV), F32), params)
            o_s, s_s = _gla_core(proj, lrp, s_blk0, sb, s_nb, sseq, 1, sseq, state_gla[ic], params)
            outs["pc"].append(s_p)
            outs["sc"].append(s_s)
            ob = jnp.concatenate([o_p, o_s], axis=0)
            x = _matmul(ob, gla_w_out, ic, res=x)
            ic += 1

        hx = _rmsnorm(x, g_xa, i, BF16)
        qp = _matmul(hx, xa_w_q, i, out_dtype=BF16, bn=XA_DIM)
        o_p = _xattn(qp[:mp].reshape(pb, pseq, XA_DIM), p_mem_k, p_mem_v, i, bb=1, lq=512)
        o_s = _xattn(qp[mp:].reshape(sb, sseq, XA_DIM), c_mem_k, c_mem_v, i, bb=8, lq=sseq)
        oc = jnp.concatenate([o_p.reshape(mp, XA_DIM), o_s.reshape(ms, XA_DIM)], axis=0)
        x = _matmul(oc, xa_w_o, i, res=x)

        hm = _rmsnorm(x, g_mlp, i, BF16)
        hid = _matmul(hm, mlp_w_up, i, mode="relu2", out_dtype=BF16)
        x = _matmul(hid, mlp_w_down, i, res=x, bm=1152, bn=1024, bk=512)

    y = _rmsnorm(x, norm_final.reshape(1, 1, d), 0, F32)
    y_prompt = y[:mp].reshape(pb, pseq, d)
    y_sample = y[mp:].reshape(sb, sseq, d)
    shp5 = (DEPTH, -1, MEM_LEN, XA_HEADS, XA_HEAD_DIM)
    return (y_prompt, y_sample,
            jnp.stack(outs["pa_re"]), jnp.stack(outs["pa_im"]), jnp.stack(outs["pb"]),
            jnp.stack(outs["pbc"]), jnp.stack(outs["pc"]),
            p_mem_k.reshape(shp5), p_mem_v.reshape(shp5),
            jnp.stack(outs["sa_re"]), jnp.stack(outs["sa_im"]), jnp.stack(outs["sb"]),
            jnp.stack(outs["sbc"]), jnp.stack(outs["sc"]))
```

```python
import functools
import math

import jax
import jax.numpy as jnp
from jax import lax
from jax.experimental import pallas as pl
from jax.experimental.pallas import tpu as pltpu

F32 = jnp.float32
BF16 = jnp.bfloat16
EPS = 1e-5

D_MODEL = 4096
DEPTH = 4
LANES = 128
ROWS = 128
VMEM_LIMIT = 56 * 1024 * 1024

S5_GROUP = 16
S5_GROUPS = D_MODEL // S5_GROUP
S5_STATE = 64
S5_GPB = LANES // S5_GROUP
S5_BLOCKS = D_MODEL // LANES
S5_SW = S5_GPB * S5_STATE

SSD_INNER = 2 * D_MODEL
SSD_HEAD_DIM = 64
SSD_HEADS = SSD_INNER // SSD_HEAD_DIM
SSD_GROUPS = 8
SSD_HPG = SSD_HEADS // SSD_GROUPS
SSD_STATE = 128
SSD_CONV = 4
SSD_GN = SSD_GROUPS * SSD_STATE
SSD_CONV_DIM = SSD_INNER + 2 * SSD_GN
SSD_IN_DIM = SSD_INNER + SSD_CONV_DIM + SSD_HEADS
SSD_GW = SSD_INNER // SSD_GROUPS

GLA_HEADS = 8
GLA_DK = D_MODEL // 2 // GLA_HEADS
GLA_DV = D_MODEL // GLA_HEADS
GLA_KEY = GLA_HEADS * GLA_DK
GLA_VAL = GLA_HEADS * GLA_DV
GLA_RANK = 16
GLA_TAU = 16.0
GLA_MAIN = 2 * GLA_KEY + 2 * GLA_VAL
GLA_SUB = 16

MEM_LEN = 256
XA_HEADS = 4
XA_HEAD_DIM = 128
XA_DIM = XA_HEADS * XA_HEAD_DIM
MLP_HIDDEN = 4 * D_MODEL


def _cparams(n_axes, vmem=VMEM_LIMIT):
    return pltpu.CompilerParams(dimension_semantics=("arbitrary",) * n_axes,
                                vmem_limit_bytes=vmem)


def _sigmoid(x):
    return 1.0 / (1.0 + jnp.exp(-x))


def _softplus(x):
    return jnp.maximum(x, 0.0) + jnp.log1p(jnp.exp(-jnp.abs(x)))


def _split3(x):
    hi = x.astype(BF16)
    r1 = x - hi.astype(F32)
    mid = r1.astype(BF16)
    lo = (r1 - mid.astype(F32)).astype(BF16)
    return hi, mid, lo


def _dot(a, b):
    return jnp.dot(a, b, preferred_element_type=F32)


def _dot_nt(a, b):
    return lax.dot_general(a, b, (((1,), (1,)), ((), ())), preferred_element_type=F32)


def _dot3_left(m01, x):
    return sum(_dot(m01, p) for p in _split3(x))


def _dot3_right(x, m01):
    return sum(_dot(p, m01) for p in _split3(x))


def _shared_out(kernel_fn, n_in, out_buf):
    if out_buf is None:
        return kernel_fn, [], [], {}

    def body(*refs):
        return kernel_fn(*refs[:n_in], *refs[n_in + 1:])

    return body, [out_buf], [pl.BlockSpec(memory_space=pl.ANY)], {n_in: 0}


def _rmsnorm_kernel(x_ref, g_ref, o_ref):
    x = x_ref[...].astype(F32)
    ms = jnp.mean(x * x, axis=-1, keepdims=True)
    o_ref[...] = (x * lax.rsqrt(ms + EPS) * g_ref[...]).astype(o_ref.dtype)


def _rmsnorm(x, g_stack, layer, out_dtype, bm=256):
    m, d = x.shape
    return pl.pallas_call(
        _rmsnorm_kernel,
        grid=(m // bm,),
        in_specs=[pl.BlockSpec((bm, d), lambda i: (i, 0)),
                  pl.BlockSpec((None, 1, d), lambda i: (layer, 0, 0))],
        out_specs=pl.BlockSpec((bm, d), lambda i: (i, 0)),
        out_shape=jax.ShapeDtypeStruct((m, d), out_dtype),
        compiler_params=_cparams(1),
        name="rmsnorm",
    )(x, g_stack)


def _epilogue(mode, acc, acc2):
    if mode == "relu2":
        r = jnp.maximum(acc, 0.0)
        return r * r
    if mode == "glu":
        return acc * _sigmoid(acc2)
    return acc


def _mm_kernel_fullk(*refs, mode, has_res):
    it = iter(refs)
    x_ref = next(it)
    w_ref = next(it)
    w2_ref = next(it) if mode == "glu" else None
    res_ref = next(it) if has_res else None
    o_ref = next(it)
    wbf_ref = next(it)
    wbf2_ref = next(it) if mode == "glu" else None

    @pl.when(pl.program_id(1) == 0)
    def _():
        wbf_ref[...] = w_ref[...].astype(BF16)
        if mode == "glu":
            wbf2_ref[...] = w2_ref[...].astype(BF16)

    x = x_ref[...].astype(BF16)
    acc = _dot(x, wbf_ref[...])
    acc2 = _dot(x, wbf2_ref[...]) if mode == "glu" else None
    out = _epilogue(mode, acc, acc2)
    if has_res:
        out = out + res_ref[...]
    o_ref[...] = out.astype(o_ref.dtype)


def _mm_kernel_splitk(*refs, mode, has_res, nk):
    it = iter(refs)
    x_ref = next(it)
    w_ref = next(it)
    res_ref = next(it) if has_res else None
    o_ref = next(it)
    acc_ref = next(it)
    k = pl.program_id(2)

    @pl.when(k == 0)
    def _():
        acc_ref[...] = jnp.zeros_like(acc_ref)

    acc_ref[...] += _dot(x_ref[...].astype(BF16), w_ref[...].astype(BF16))

    @pl.when(k == nk - 1)
    def _():
        out = _epilogue(mode, acc_ref[...], None)
        if has_res:
            out = out + res_ref[...]
        o_ref[...] = out.astype(o_ref.dtype)


def _matmul(x, w, layer, *, n_out=None, w2=None, res=None, mode="plain", out_dtype=F32,
            bm=512, bn=512, bk=None):
    m, kdim = x.shape
    n = n_out if n_out is not None else w.shape[2]
    bk = kdim if bk is None else bk
    nk = kdim // bk
    has_res = res is not None
    assert m % bm == 0 and n % bn == 0 and kdim % bk == 0
    if nk == 1:
        grid = (n // bn, m // bm)
        in_specs = [pl.BlockSpec((bm, kdim), lambda j, i: (i, 0)),
                    pl.BlockSpec((None, kdim, bn), lambda j, i: (layer, 0, j))]
        args = [x, w]
        scratch = [pltpu.VMEM((kdim, bn), BF16)]
        if mode == "glu":
            in_specs.append(pl.BlockSpec((None, kdim, bn), lambda j, i: (layer, 0, j)))
            args.append(w2)
            scratch.append(pltpu.VMEM((kdim, bn), BF16))
        if has_res:
            in_specs.append(pl.BlockSpec((bm, bn), lambda j, i: (i, j)))
            args.append(res)
        out_spec = pl.BlockSpec((bm, bn), lambda j, i: (i, j))
        body = functools.partial(_mm_kernel_fullk, mode=mode, has_res=has_res)
        n_axes = 2
    else:
        assert mode != "glu"
        grid = (n // bn, m // bm, nk)
        in_specs = [pl.BlockSpec((bm, bk), lambda j, i, k: (i, k)),
                    pl.BlockSpec((None, bk, bn), lambda j, i, k: (layer, k, j))]
        args = [x, w]
        if has_res:
            in_specs.append(pl.BlockSpec((bm, bn), lambda j, i, k: (i, j)))
            args.append(res)
        out_spec = pl.BlockSpec((bm, bn), lambda j, i, k: (i, j))
        scratch = [pltpu.VMEM((bm, bn), F32)]
        body = functools.partial(_mm_kernel_splitk, mode=mode, has_res=has_res, nk=nk)
        n_axes = 3
    return pl.pallas_call(
        body, grid=grid, in_specs=in_specs, out_specs=out_spec,
        out_shape=jax.ShapeDtypeStruct((m, n), out_dtype),
        scratch_shapes=scratch, compiler_params=_cparams(n_axes),
        name="matmul_" + mode,
    )(*args)


def _xattn_kernel(q_ref, k_ref, v_ref, o_ref, *, bb, lq):
    scale = XA_HEAD_DIM ** -0.5
    for ib in range(bb):
        rows = slice(ib * lq, (ib + 1) * lq)
        for h in range(XA_HEADS):
            sl = slice(h * XA_HEAD_DIM, (h + 1) * XA_HEAD_DIM)
            q = q_ref[rows, sl].astype(BF16)
            k = k_ref[ib, :, sl].astype(BF16)
            v = v_ref[ib, :, sl].astype(BF16)
            s = _dot_nt(q, k) * scale
            p = jnp.exp(s - jnp.max(s, axis=-1, keepdims=True))
            l = jnp.sum(p, axis=-1, keepdims=True)
            o_ref[rows, sl] = (_dot(p.astype(BF16), v) / l).astype(o_ref.dtype)


def _xattn(q, mk, mv, layer, row0, nseq, seq, *, bb, lq, out_buf=None):
    assert bb == 1 or lq == seq
    brows = bb * lq
    assert row0 % brows == 0
    nq = seq // lq
    blk0 = row0 // brows

    def qblk(i, j):
        return (blk0 + i * nq + j, 0)

    in_specs = [pl.BlockSpec((brows, XA_DIM), qblk),
                pl.BlockSpec((None, bb, MEM_LEN, XA_DIM), lambda i, j: (layer, i, 0, 0)),
                pl.BlockSpec((None, bb, MEM_LEN, XA_DIM), lambda i, j: (layer, i, 0, 0))]
    body, extra_args, extra_specs, aliases = _shared_out(
        functools.partial(_xattn_kernel, bb=bb, lq=lq), len(in_specs), out_buf)
    return pl.pallas_call(
        body,
        grid=(nseq // bb, nq),
        input_output_aliases=aliases,
        in_specs=in_specs + extra_specs,
        out_specs=pl.BlockSpec((brows, XA_DIM), qblk),
        out_shape=jax.ShapeDtypeStruct(q.shape, F32),
        compiler_params=_cparams(2),
        name="xattn",
    )(q, mk, mv, *extra_args)


def _s5_kernel(u_ref, h0r_ref, h0i_ref, ktr_ref, bp_ref, cp_ref, at_ref, d_ref,
               act_ref, hr_ref, hi_ref, ucat_ref, s_ref, hin_ref, ybuf_ref, *, nb, nc, t_len):
    r = nb * nc
    for s in range(t_len):
        ucat_ref[:, s * LANES:(s + 1) * LANES] = u_ref[pl.ds(s, r, stride=t_len), :].astype(BF16)
    half = S5_SW // LANES
    s_all = _dot(ucat_ref[...], bp_ref[0])
    for k in range(2 * half):
        s_ref[k] = s_all[:, k * LANES:(k + 1) * LANES]
    a_sl = [at_ref[0, :, k * LANES:(k + 1) * LANES] for k in range(2 * half)]

    def body(c, carry):
        rows_c = pl.ds(c, nb, stride=nc)
        new = []
        for k in range(half):
            hr, hi = carry[k], carry[half + k]
            hin_ref[k, rows_c, :] = hr
            hin_ref[half + k, rows_c, :] = hi
            new.append((a_sl[k] * hr - a_sl[half + k] * hi + s_ref[k, rows_c, :],
                        a_sl[k] * hi + a_sl[half + k] * hr + s_ref[half + k, rows_c, :]))
        return tuple(n[0] for n in new) + tuple(n[1] for n in new)

    h0 = (tuple(h0r_ref[:, k * LANES:(k + 1) * LANES] for k in range(half))
          + tuple(h0i_ref[:, k * LANES:(k + 1) * LANES] for k in range(half)))
    hfin = lax.fori_loop(0, nc, body, h0)
    for k in range(half):
        hr_ref[:, k * LANES:(k + 1) * LANES] = hfin[k]
        hi_ref[:, k * LANES:(k + 1) * LANES] = hfin[half + k]
    hin = jnp.concatenate([hin_ref[k] for k in range(2 * half)], axis=1).astype(BF16)
    for t in range(t_len):
        y = _dot(hin, cp_ref[0, t])
        y = y + _dot(ucat_ref[:, :(t + 1) * LANES], ktr_ref[0, (t_len - 1 - t) * LANES:, :])
        ut = u_ref[pl.ds(t, r, stride=t_len), :]
        y = y + d_ref[...] * ut
        ybuf_ref[pl.ds(t, r, stride=t_len), :] = jax.nn.gelu(y)
    act_ref[...] = ybuf_ref[...].astype(act_ref.dtype)


def _s5_prep(a_re, a_im, log_dt, b_re, b_im, c_re, c_im, t_len):
    hp = lax.Precision.HIGHEST
    g, p = a_re.shape
    lam_re = a_re.astype(F32)
    lam_im = a_im.astype(F32)
    dt = jnp.exp(log_dt.astype(F32))[:, None]
    kk = jnp.arange(t_len + 1, dtype=F32)[:, None, None]
    magk = jnp.exp(kk * (lam_re * dt))
    angk = kk * (lam_im * dt)
    pw_re = magk * jnp.cos(angk)
    pw_im = magk * jnp.sin(angk)
    ab_re, ab_im = pw_re[1], pw_im[1]
    den = lam_re * lam_re + lam_im * lam_im
    f_re = ((ab_re - 1.0) * lam_re + ab_im * lam_im) / den
    f_im = (ab_im * lam_re - (ab_re - 1.0) * lam_im) / den
    br = b_re.astype(F32)
    bi = b_im.astype(F32)
    bb_re = f_re[..., None] * br - f_im[..., None] * bi
    bb_im = f_re[..., None] * bi + f_im[..., None] * br
    cr = c_re.astype(F32)
    ci = c_im.astype(F32)
    eye = jnp.eye(S5_GPB, dtype=F32)
    nblk = g // S5_GPB

    kr = jnp.float32(t_len - 1) - jnp.arange(t_len, dtype=F32)[:, None, None]
    magr = jnp.exp(kr * (lam_re * dt))
    angr = kr * (lam_im * dt)
    rev = magr * jnp.cos(angr), magr * jnp.sin(angr)
    v_re = rev[0][..., None] * bb_re - rev[1][..., None] * bb_im
    v_im = rev[0][..., None] * bb_im + rev[1][..., None] * bb_re
    vc = jnp.stack([v_re, v_im], axis=0)
    vc = vc.reshape(2, t_len, nblk, S5_GPB, p, S5_GROUP)
    vt = jnp.transpose(vc, (2, 1, 3, 5, 0, 4))
    bp = (vt[:, :, :, :, :, None, :] * eye.reshape(1, 1, S5_GPB, 1, 1, S5_GPB, 1)).astype(BF16)
    bp = bp.reshape(nblk, t_len * LANES, 2 * S5_SW)

    w_re = cr[None] * pw_re[1:, :, None, :] - ci[None] * pw_im[1:, :, None, :]
    w_im = cr[None] * pw_im[1:, :, None, :] + ci[None] * pw_re[1:, :, None, :]
    wc = jnp.stack([w_re, -w_im], axis=0).reshape(2, t_len, nblk, S5_GPB, S5_GROUP, p)
    wt = jnp.transpose(wc, (2, 1, 0, 5, 3, 4))
    cp = (wt[:, :, :, None, :, :, :] * eye.reshape(1, 1, 1, S5_GPB, 1, S5_GPB, 1)).astype(BF16)
    cp = cp.reshape(nblk, t_len, 2 * S5_SW, LANES)

    kt = (jnp.einsum('gcp,tgpd->tgcd', cr, v_re, precision=hp)
          - jnp.einsum('gcp,tgpd->tgcd', ci, v_im, precision=hp))
    kt = kt.reshape(t_len, nblk, S5_GPB, S5_GROUP, S5_GROUP)
    ktt = jnp.transpose(kt, (1, 0, 2, 4, 3))
    ktr = (ktt[:, :, :, :, None, :] * eye.reshape(1, 1, S5_GPB, 1, S5_GPB, 1)).astype(BF16)
    ktr = ktr.reshape(nblk, t_len * LANES, LANES)

    at = jnp.concatenate([pw_re[t_len].reshape(nblk, 1, S5_SW),
                          pw_im[t_len].reshape(nblk, 1, S5_SW)], axis=2)
    return ktr, bp, cp, at


def _s5_core(u, row0, nb, seq, t_len, h0_re, h0_im, ops, d_skip, out_buf=None):
    ktr, bp, cp, at = ops
    nc = seq // t_len
    r = nb * nc
    nrows = nb * seq
    assert row0 % nrows == 0
    rb = row0 // nrows
    h0r = h0_re.reshape(nb, S5_GROUPS * S5_STATE)
    h0i = h0_im.reshape(nb, S5_GROUPS * S5_STATE)
    out_shapes = (jax.ShapeDtypeStruct((u.shape[0], D_MODEL), BF16),
                  jax.ShapeDtypeStruct((nb, S5_GROUPS * S5_STATE), F32),
                  jax.ShapeDtypeStruct((nb, S5_GROUPS * S5_STATE), F32))
    body, extra_args, extra_specs, aliases = _shared_out(
        functools.partial(_s5_kernel, nb=nb, nc=nc, t_len=t_len), 8, out_buf)
    act, hr, hi = pl.pallas_call(
        body,
        grid=(S5_BLOCKS,),
        input_output_aliases=aliases,
        in_specs=[pl.BlockSpec((nrows, LANES), lambda j: (rb, j)),
                  pl.BlockSpec((nb, S5_SW), lambda j: (0, j)),
                  pl.BlockSpec((nb, S5_SW), lambda j: (0, j)),
                  pl.BlockSpec((1, t_len * LANES, LANES), lambda j: (j, 0, 0)),
                  pl.BlockSpec((1, t_len * LANES, 2 * S5_SW), lambda j: (j, 0, 0)),
                  pl.BlockSpec((1, t_len, 2 * S5_SW, LANES), lambda j: (j, 0, 0, 0)),
                  pl.BlockSpec((1, 1, 2 * S5_SW), lambda j: (j, 0, 0)),
                  pl.BlockSpec((1, LANES), lambda j: (0, j))] + extra_specs,
        out_specs=(pl.BlockSpec((nrows, LANES), lambda j: (rb, j)),
                   pl.BlockSpec((nb, S5_SW), lambda j: (0, j)),
                   pl.BlockSpec((nb, S5_SW), lambda j: (0, j))),
        out_shape=out_shapes,
        scratch_shapes=[pltpu.VMEM((r, t_len * LANES), BF16),
                        pltpu.VMEM((2 * S5_SW // LANES, r, LANES), F32),
                        pltpu.VMEM((2 * S5_SW // LANES, r, LANES), F32),
                        pltpu.VMEM((nrows, LANES), F32)],
        compiler_params=_cparams(1),
        name="s5_core",
    )(u, h0r, h0i, ktr, bp, cp, at, d_skip.reshape(1, D_MODEL), *extra_args)
    return (act, hr.reshape(nb, S5_GROUPS, S5_STATE), hi.reshape(nb, S5_GROUPS, S5_STATE))


def _ssd_kernel(z_ref, x_ref, b_ref, c_ref, dt_ref, dtt_ref,
                cwx_ref, cwb_ref, cwc_ref, cbx_ref, cbb_ref, cbc_ref,
                csx_ref, csb_ref, csc_ref,
                dtb_ref, dtbt_ref, alog_ref, alogt_ref, dexp_ref, norm_ref, st0_ref,
                y_ref, st_ref, xbuf, bbuf, cbuf, *, nb, qt):
    rows = nb * qt
    c = pl.program_id(2)
    pad = 8
    hist = SSD_CONV - 1

    @pl.when(c == 0)
    def _():
        st_ref[...] = st0_ref[...]
        xbuf[:, pad - hist:pad, :] = csx_ref[...]
        bbuf[:, pad - hist:pad, :] = csb_ref[...]
        cbuf[:, pad - hist:pad, :] = csc_ref[...]

    def conv(buf, raw_ref, w_ref, bias_ref, width):
        buf[:, pad:, :] = raw_ref[...].reshape(nb, qt, width)
        acc = bias_ref[...].reshape(1, 1, width)
        for k in range(SSD_CONV):
            acc = acc + buf[:, pad - hist + k:pad - hist + k + qt, :] * w_ref[k:k + 1, :].reshape(1, 1, width)
        tail = buf[:, pad + qt - hist:pad + qt, :]
        buf[:, pad - hist:pad, :] = tail
        acc = acc * _sigmoid(acc)
        return acc.reshape(rows, width)

    xs = conv(xbuf, x_ref, cwx_ref, cbx_ref, SSD_GW)
    bs = conv(bbuf, b_ref, cwb_ref, cbb_ref, SSD_STATE).astype(BF16)
    cs = conv(cbuf, c_ref, cwc_ref, cbc_ref, SSD_STATE).astype(BF16)

    dt = _softplus(dt_ref[0] + dtb_ref[0])
    dtt = _softplus(dtt_ref[0] + dtbt_ref[0])
    da = dt * (-jnp.exp(alog_ref[0]))
    dat = dtt * (-jnp.exp(alogt_ref[0]))

    ri = lax.broadcasted_iota(jnp.int32, (rows, rows), 0)
    ci = lax.broadcasted_iota(jnp.int32, (rows, rows), 1)
    same = (ri // qt) == (ci // qt)
    tri = same & (ri >= ci)
    tri_bf = tri.astype(BF16)
    trit_bf = (same & (ri <= ci)).astype(BF16)
    same_bf = same.astype(BF16)
    cum = _dot3_left(tri_bf, da)
    cumt = _dot3_right(dat, trit_bf)
    tot = _dot3_left(same_bf, da)
    cb = _dot_nt(cs, bs)

    lane = lax.broadcasted_iota(jnp.int32, (1, LANES), 1)
    lo = lane < SSD_HEAD_DIM
    srow = lax.broadcasted_iota(jnp.int32, (LANES, 1), 0) < SSD_HEAD_DIM
    row_seq = lax.broadcasted_iota(jnp.int32, (rows, 1), 0) // qt
    neg_inf = jnp.float32(-jnp.inf)
    bs_seq = [bs if nb == 1 else jnp.where(row_seq == ib, bs, jnp.zeros_like(bs)) for ib in range(nb)]

    ys = []
    for p in range(SSD_HPG // 2):
        ra, rb = 2 * p, 2 * p + 1
        ws = []
        for r in (ra, rb):
            seg = cum[:, r:r + 1] - cumt[r:r + 1, :]
            dec = jnp.exp(jnp.where(tri, seg, neg_inf))
            ws.append((cb * dec * dtt[r:r + 1, :]).astype(BF16))
        xp = xs[:, p * LANES:(p + 1) * LANES]
        y = (_dot(ws[0], jnp.where(lo, xp, 0.0).astype(BF16))
             + _dot(ws[1], jnp.where(lo, 0.0, xp).astype(BF16)))
        ecol = jnp.where(lo, jnp.exp(cum[:, ra:ra + 1]), jnp.exp(cum[:, rb:rb + 1]))
        inter = []
        for ib in range(nb):
            st = st_ref[ib, 0, p * LANES:(p + 1) * LANES, :].astype(BF16)
            inter.append(_dot_nt(cs[ib * qt:(ib + 1) * qt, :], st))
        inter = inter[0] if nb == 1 else jnp.concatenate(inter, axis=0)
        y = y + inter * ecol + dexp_ref[:, p * LANES:(p + 1) * LANES] * xp
        ys.append(y)

        sc = jnp.where(lo, jnp.exp(tot[:, ra:ra + 1] - cum[:, ra:ra + 1]) * dt[:, ra:ra + 1],
                       jnp.exp(tot[:, rb:rb + 1] - cum[:, rb:rb + 1]) * dt[:, rb:rb + 1])
        xwt = jnp.transpose(xp * sc).astype(BF16)
        for ib in range(nb):
            upd = _dot(xwt, bs_seq[ib])
            r0 = ib * qt
            dcol = jnp.where(srow, jnp.exp(tot[r0:r0 + 1, ra:ra + 1]), jnp.exp(tot[r0:r0 + 1, rb:rb + 1]))
            old = st_ref[ib, 0, p * LANES:(p + 1) * LANES, :]
            st_ref[ib, 0, p * LANES:(p + 1) * LANES, :] = old * dcol + upd

    y = jnp.concatenate(ys, axis=1)
    zg = z_ref[...]
    y = y * (zg * _sigmoid(zg))
    ms = jnp.mean(y * y, axis=-1, keepdims=True)
    y_ref[...] = (y * lax.rsqrt(ms + EPS) * norm_ref[...]).astype(y_ref.dtype)


def _ssd_core(proj, dtp, dtt, row_blk0, nseq, nb, qt, nc, st0, conv0, params, out_buf=None):
    cw, cbias, dtb, dtbt, alog, alogt, dexp, norm = params
    nbb = nseq // nb
    xoff = SSD_INNER // SSD_GW
    boff = (SSD_INNER + SSD_INNER) // SSD_STATE
    coff = boff + SSD_GN // SSD_STATE
    cwb_off = SSD_INNER // SSD_STATE
    cwc_off = cwb_off + SSD_GN // SSD_STATE

    def rblk(i, g, c):
        return row_blk0 + i * nc + c

    st0 = st0.reshape(nseq, SSD_GROUPS, SSD_GW, SSD_STATE)
    hist = SSD_CONV - 1
    in_specs = [
        pl.BlockSpec((ROWS, SSD_GW), lambda i, g, c: (rblk(i, g, c), g)),
        pl.BlockSpec((ROWS, SSD_GW), lambda i, g, c: (rblk(i, g, c), xoff + g)),
        pl.BlockSpec((ROWS, SSD_STATE), lambda i, g, c: (rblk(i, g, c), boff + g)),
        pl.BlockSpec((ROWS, SSD_STATE), lambda i, g, c: (rblk(i, g, c), coff + g)),
        pl.BlockSpec((1, ROWS, LANES), lambda i, g, c: (g, rblk(i, g, c), 0)),
        pl.BlockSpec((1, SSD_HPG, ROWS), lambda i, g, c: (g, 0, rblk(i, g, c))),
        pl.BlockSpec((SSD_CONV, SSD_GW), lambda i, g, c: (0, g)),
        pl.BlockSpec((SSD_CONV, SSD_STATE), lambda i, g, c: (0, cwb_off + g)),
        pl.BlockSpec((SSD_CONV, SSD_STATE), lambda i, g, c: (0, cwc_off + g)),
        pl.BlockSpec((1, SSD_GW), lambda i, g, c: (0, g)),
        pl.BlockSpec((1, SSD_STATE), lambda i, g, c: (0, cwb_off + g)),
        pl.BlockSpec((1, SSD_STATE), lambda i, g, c: (0, cwc_off + g)),
        pl.BlockSpec((nb, hist, SSD_GW), lambda i, g, c: (i, 0, g)),
        pl.BlockSpec((nb, hist, SSD_STATE), lambda i, g, c: (i, 0, cwb_off + g)),
        pl.BlockSpec((nb, hist, SSD_STATE), lambda i, g, c: (i, 0, cwc_off + g)),
        pl.BlockSpec((1, 1, LANES), lambda i, g, c: (g, 0, 0)),
        pl.BlockSpec((1, SSD_HPG, 1), lambda i, g, c: (g, 0, 0)),
        pl.BlockSpec((1, 1, LANES), lambda i, g, c: (g, 0, 0)),
        pl.BlockSpec((1, SSD_HPG, 1), lambda i, g, c: (g, 0, 0)),
        pl.BlockSpec((1, SSD_GW), lambda i, g, c: (0, g)),
        pl.BlockSpec((1, SSD_GW), lambda i, g, c: (0, g)),
        pl.BlockSpec((nb, 1, SSD_GW, SSD_STATE), lambda i, g, c: (i, g, 0, 0)),
    ]
    out_specs = (pl.BlockSpec((ROWS, SSD_GW), lambda i, g, c: (rblk(i, g, c), g)),
                 pl.BlockSpec((nb, 1, SSD_GW, SSD_STATE), lambda i, g, c: (i, g, 0, 0)))
    out_shape = (jax.ShapeDtypeStruct((proj.shape[0], SSD_INNER), BF16),
                 jax.ShapeDtypeStruct((nseq, SSD_GROUPS, SSD_GW, SSD_STATE), F32))
    body, extra_args, extra_specs, aliases = _shared_out(
        functools.partial(_ssd_kernel, nb=nb, qt=qt), len(in_specs), out_buf)
    y, st = pl.pallas_call(
        body,
        grid=(nbb, SSD_GROUPS, nc),
        input_output_aliases=aliases,
        in_specs=in_specs + extra_specs, out_specs=out_specs, out_shape=out_shape,
        scratch_shapes=[pltpu.VMEM((nb, 8 + qt, SSD_GW), F32),
                        pltpu.VMEM((nb, 8 + qt, SSD_STATE), F32),
                        pltpu.VMEM((nb, 8 + qt, SSD_STATE), F32)],
        compiler_params=_cparams(3),
        name="ssd_core",
    )(proj, proj, proj, proj, dtp, dtt, cw, cw, cw, cbias, cbias, cbias,
      conv0, conv0, conv0, dtb, dtbt, alog, alogt, dexp, norm, st0, *extra_args)
    return y, st.reshape(nseq, SSD_HEADS, SSD_HEAD_DIM, SSD_STATE)


def _gla_kernel(q_ref, k_ref, v_ref, r_ref, lr_ref, wa2_ref, ba_ref, norm_ref, s0_ref,
                o_ref, s_ref, cum_s, q_s, k_s, att_s, *, nb, qt, wq):
    rows = nb * qt
    c = pl.program_id(2)

    @pl.when(c == 0)
    def _():
        s_ref[...] = s0_ref[...]

    qv = q_ref[...] * (GLA_DK ** -0.5)
    kv = k_ref[...]
    v_bf = v_ref[...].astype(BF16)
    gate = _dot(lr_ref[...].astype(BF16), wa2_ref[...].astype(BF16)) + ba_ref[...]
    g = -_softplus(-gate) / GLA_TAU

    ri = lax.broadcasted_iota(jnp.int32, (rows, rows), 0)
    ci = lax.broadcasted_iota(jnp.int32, (rows, rows), 1)
    same = (ri // qt) == (ci // qt)
    tri_bf = (same & (ri >= ci)).astype(BF16)
    same_bf = same.astype(BF16)
    cum = _dot3_left(tri_bf, g)
    tot = _dot3_left(same_bf, g)
    cum_s[...] = cum
    q_s[...] = qv
    k_s[...] = kv

    qe = (qv * jnp.exp(cum)).astype(BF16)
    o_parts = [_dot(qe[ib * qt:(ib + 1) * qt, :], s_ref[ib, 0].astype(BF16)) for ib in range(nb)]
    o = o_parts[0] if nb == 1 else jnp.concatenate(o_parts, axis=0)

    neg_inf = jnp.float32(-jnp.inf)
    att_s[...] = jnp.zeros_like(att_s)
    if qt > wq:
        assert nb == 1
        rowi = lax.broadcasted_iota(jnp.int32, (rows, 1), 0)
        for blk in range(1, qt // wq):
            lo_r, hi_r = blk * wq, (blk + 1) * wq
            c0 = cum[lo_r - 1:lo_r, :]
            q_blk = (qv[lo_r:hi_r, :] * jnp.exp(cum[lo_r:hi_r, :] - c0)).astype(BF16)
            k_all = (kv * jnp.exp(jnp.where(rowi < lo_r, c0 - cum, neg_inf))).astype(BF16)
            att_s[lo_r:hi_r, :] = _dot_nt(q_blk, k_all)

    lane = lax.broadcasted_iota(jnp.int32, (1, rows), 1)
    wrow = lax.broadcasted_iota(jnp.int32, (wq, 1), 0)

    def window(w, carry):
        w0 = pl.multiple_of(w * wq, wq)
        cw = cum_s[pl.ds(w0, wq), :]
        qw = q_s[pl.ds(w0, wq), :]
        kw = k_s[pl.ds(w0, wq), :]
        blk = att_s[pl.ds(w0, wq), :]
        for jj in range(wq):
            e = jnp.exp(jnp.where(wrow >= jj, cw - cw[jj:jj + 1, :], neg_inf))
            col = jnp.sum(qw * e * kw[jj:jj + 1, :], axis=-1, keepdims=True)
            blk = jnp.where(lane == w0 + jj, col, blk)
        att_s[pl.ds(w0, wq), :] = blk
        return carry

    lax.fori_loop(0, rows // wq, window, 0)
    o = o + _dot(att_s[...].astype(BF16), v_bf)

    kdt = jnp.transpose(kv * jnp.exp(tot - cum)).astype(BF16)
    gt = jnp.transpose(g)
    rsel = lax.broadcasted_iota(jnp.int32, (rows, LANES), 0) // qt
    row_seq = lax.broadcasted_iota(jnp.int32, (rows, 1), 0) // qt
    for ib in range(nb):
        sel = (rsel == ib).astype(BF16)
        dec = jnp.exp(_dot3_right(gt, sel))
        dec = jnp.concatenate([dec] * (GLA_DV // LANES), axis=1)
        v_ib = v_bf if nb == 1 else jnp.where(row_seq == ib, v_bf, jnp.zeros_like(v_bf))
        s_ref[ib, 0] = s_ref[ib, 0] * dec + _dot(kdt, v_ib)

    ms = jnp.mean(o * o, axis=-1, keepdims=True)
    o = o * lax.rsqrt(ms + EPS) * norm_ref[...]
    rr = r_ref[...]
    o_ref[...] = (o * (rr * _sigmoid(rr))).astype(o_ref.dtype)


def _gla_core(proj, lrp, row_blk0, nseq, nb, qt, nc, wq, s0, params, out_buf=None):
    wa2p, ba, norm = params
    nbb = nseq // nb
    koff = GLA_KEY // GLA_DK
    voff = 2 * GLA_KEY // GLA_DV
    roff = voff + GLA_VAL // GLA_DV

    def rblk(i, h, c):
        return row_blk0 + i * nc + c

    in_specs = [
        pl.BlockSpec((ROWS, GLA_DK), lambda i, h, c: (rblk(i, h, c), h)),
        pl.BlockSpec((ROWS, GLA_DK), lambda i, h, c: (rblk(i, h, c), koff + h)),
        pl.BlockSpec((ROWS, GLA_DV), lambda i, h, c: (rblk(i, h, c), voff + h)),
        pl.BlockSpec((ROWS, GLA_DV), lambda i, h, c: (rblk(i, h, c), roff + h)),
        pl.BlockSpec((ROWS, LANES), lambda i, h, c: (rblk(i, h, c), 0)),
        pl.BlockSpec((LANES, GLA_DK), lambda i, h, c: (0, h)),
        pl.BlockSpec((1, GLA_DK), lambda i, h, c: (0, h)),
        pl.BlockSpec((1, GLA_DV), lambda i, h, c: (0, 0)),
        pl.BlockSpec((nb, 1, GLA_DK, GLA_DV), lambda i, h, c: (i, h, 0, 0)),
    ]
    out_specs = (pl.BlockSpec((ROWS, GLA_DV), lambda i, h, c: (rblk(i, h, c), h)),
                 pl.BlockSpec((nb, 1, GLA_DK, GLA_DV), lambda i, h, c: (i, h, 0, 0)))
    out_shape = (jax.ShapeDtypeStruct((proj.shape[0], GLA_VAL), BF16),
                 jax.ShapeDtypeStruct((nseq, GLA_HEADS, GLA_DK, GLA_DV), F32))
    body, extra_args, extra_specs, aliases = _shared_out(
        functools.partial(_gla_kernel, nb=nb, qt=qt, wq=wq), len(in_specs), out_buf)
    return pl.pallas_call(
        body,
        grid=(nbb, GLA_HEADS, nc),
        input_output_aliases=aliases,
        in_specs=in_specs + extra_specs, out_specs=out_specs, out_shape=out_shape,
        scratch_shapes=[pltpu.VMEM((ROWS, GLA_DK), F32), pltpu.VMEM((ROWS, GLA_DK), F32),
                        pltpu.VMEM((ROWS, GLA_DK), F32), pltpu.VMEM((ROWS, ROWS), F32)],
        compiler_params=_cparams(3),
        name="gla_core",
    )(proj, proj, proj, proj, lrp, wa2p, ba, norm, s0, *extra_args)


def kernel(x_prompt, x_sample, state_s5_re, state_s5_im, state_ssd, state_ssd_conv, state_gla, cache_mem_k, cache_mem_v, mem_prompt, norm_mix, norm_xattn, norm_mem, norm_mlp, norm_final, xa_w_q, xa_w_k, xa_w_v, xa_w_o, mlp_w_up, mlp_w_down, s5_a_re, s5_a_im, s5_log_dt, s5_b_re, s5_b_im, s5_c_re, s5_c_im, s5_d, s5_glu_w1, s5_glu_w2, ssd_w_in, ssd_conv_w, ssd_conv_b, ssd_dt_bias, ssd_a_log, ssd_d, ssd_norm, ssd_w_out, gla_w_in, gla_w_a2, gla_b_a, gla_norm, gla_w_out):
    pb, pseq, d = x_prompt.shape
    sb, sseq, _ = x_sample.shape
    mp = pb * pseq
    ms = sb * sseq
    assert mp % ROWS == 0 and ms % ROWS == 0 and pseq % ROWS == 0 and ROWS % sseq == 0
    x = jnp.concatenate([x_prompt.reshape(mp, d), x_sample.reshape(ms, d)], axis=0)
    m = mp + ms
    p_blk0 = 0
    s_blk0 = mp // ROWS
    s_nb = ROWS // sseq

    g_mix = norm_mix.reshape(DEPTH, 1, d)
    g_xa = norm_xattn.reshape(DEPTH, 1, d)
    g_mem = norm_mem.reshape(DEPTH, 1, d)
    g_mlp = norm_mlp.reshape(DEPTH, 1, d)

    mem2 = mem_prompt.reshape(pb * MEM_LEN, d)
    pk, pv = [], []
    for i in range(DEPTH):
        mn = _rmsnorm(mem2, g_mem, i, BF16)
        pk.append(_matmul(mn, xa_w_k, i, bm=512, bn=XA_DIM))
        pv.append(_matmul(mn, xa_w_v, i, bm=512, bn=XA_DIM))
    p_mem_k = jnp.stack(pk).reshape(DEPTH, pb, MEM_LEN, XA_DIM)
    p_mem_v = jnp.stack(pv).reshape(DEPTH, pb, MEM_LEN, XA_DIM)
    c_mem_k = cache_mem_k.reshape(DEPTH, sb, MEM_LEN, XA_DIM)
    c_mem_v = cache_mem_v.reshape(DEPTH, sb, MEM_LEN, XA_DIM)

    outs = {k: [] for k in ("pa_re", "pa_im", "pb", "pbc", "pc", "sa_re", "sa_im", "sb", "sbc", "sc")}
    ia = ib = ic = 0
    for i in range(DEPTH):
        kind = i % 3
        if kind == 0:
            hn = _rmsnorm(x, g_mix, i, F32)
            ops_p = _s5_prep(s5_a_re[ia], s5_a_im[ia], s5_log_dt[ia], s5_b_re[ia], s5_b_im[ia],
                             s5_c_re[ia], s5_c_im[ia], 16)
            ops_s = _s5_prep(s5_a_re[ia], s5_a_im[ia], s5_log_dt[ia], s5_b_re[ia], s5_b_im[ia],
                             s5_c_re[ia], s5_c_im[ia], sseq)
            zero = jnp.zeros((pb, S5_GROUPS, S5_STATE), F32)
            act, hr_p, hi_p = _s5_core(hn, 0, pb, pseq, 16, zero, zero, ops_p, s5_d[ia])
            act, hr_s, hi_s = _s5_core(hn, mp, sb, sseq, sseq, state_s5_re[ia], state_s5_im[ia],
                                       ops_s, s5_d[ia], out_buf=act)
            outs["pa_re"].append(hr_p)
            outs["pa_im"].append(hi_p)
            outs["sa_re"].append(hr_s)
            outs["sa_im"].append(hi_s)
            x = _matmul(act, s5_glu_w1, ia, w2=s5_glu_w2, res=x, mode="glu", bn=256)
            ia += 1
        elif kind == 1:
            hn = _rmsnorm(x, g_mix, i, BF16)
            proj = _matmul(hn, ssd_w_in, ib, bn=640)
            dt_raw = proj[:, SSD_INNER + SSD_CONV_DIM:].reshape(m, SSD_GROUPS, SSD_HPG)
            dtt = jnp.transpose(dt_raw, (1, 2, 0))
            dtp = jnp.pad(jnp.transpose(dt_raw, (1, 0, 2)), ((0, 0), (0, 0), (0, LANES - SSD_HPG)))
            pad_h = ((0, 0), (0, 0), (0, LANES - SSD_HPG))
            params = (ssd_conv_w[ib], ssd_conv_b[ib].reshape(1, SSD_CONV_DIM),
                      jnp.pad(ssd_dt_bias[ib].reshape(SSD_GROUPS, 1, SSD_HPG), pad_h),
                      ssd_dt_bias[ib].reshape(SSD_GROUPS, SSD_HPG, 1),
                      jnp.pad(ssd_a_log[ib].reshape(SSD_GROUPS, 1, SSD_HPG), pad_h),
                      ssd_a_log[ib].reshape(SSD_GROUPS, SSD_HPG, 1),
                      jnp.repeat(ssd_d[ib], SSD_HEAD_DIM).reshape(1, SSD_INNER),
                      ssd_norm[ib].reshape(1, SSD_INNER))
            yb, st_p = _ssd_core(proj, dtp, dtt, p_blk0, pb, 1, ROWS, pseq // ROWS,
                                  jnp.zeros((pb, SSD_HEADS, SSD_HEAD_DIM, SSD_STATE), F32),
                                  jnp.zeros((pb, SSD_CONV - 1, SSD_CONV_DIM), F32), params)
            yb, st_s = _ssd_core(proj, dtp, dtt, s_blk0, sb, s_nb, sseq, 1,
                                 state_ssd[ib], state_ssd_conv[ib], params, out_buf=yb)
            xbc = proj[:, SSD_INNER:SSD_INNER + SSD_CONV_DIM]
            outs["pb"].append(st_p)
            outs["sb"].append(st_s)
            outs["pbc"].append(xbc[:mp].reshape(pb, pseq, SSD_CONV_DIM)[:, pseq - (SSD_CONV - 1):])
            outs["sbc"].append(xbc[mp:].reshape(sb, sseq, SSD_CONV_DIM)[:, sseq - (SSD_CONV - 1):])
            x = _matmul(yb, ssd_w_out, ib, res=x, bm=1152, bn=1024, bk=1024)
            ib += 1
        else:
            hn = _rmsnorm(x, g_mix, i, BF16)
            proj = _matmul(hn, gla_w_in, ic, n_out=GLA_MAIN)
            w_lr = jnp.pad(gla_w_in[ic][:, GLA_MAIN:], ((0, 0), (0, LANES - GLA_RANK)))[None]
            lrp = _matmul(hn, w_lr, 0, bn=LANES)
            params = (jnp.pad(gla_w_a2[ic], ((0, LANES - GLA_RANK), (0, 0))),
                      gla_b_a[ic].reshape(1, GLA_KEY), gla_norm[ic].reshape(1, GLA_DV))
            ob, s_p = _gla_core(proj, lrp, p_blk0, pb, 1, ROWS, pseq // ROWS, GLA_SUB,
                                 jnp.zeros((pb, GLA_HEADS, GLA_DK, GLA_DV), F32), params)
            ob, s_s = _gla_core(proj, lrp, s_blk0, sb, s_nb, sseq, 1, sseq, state_gla[ic], params,
                                out_buf=ob)
            outs["pc"].append(s_p)
            outs["sc"].append(s_s)
            x = _matmul(ob, gla_w_out, ic, res=x)
            ic += 1

        hx = _rmsnorm(x, g_xa, i, BF16)
        qp = _matmul(hx, xa_w_q, i, bn=XA_DIM)
        oc = _xattn(qp, p_mem_k, p_mem_v, i, 0, pb, pseq, bb=1, lq=512)
        oc = _xattn(qp, c_mem_k, c_mem_v, i, mp, sb, sseq, bb=8, lq=sseq, out_buf=oc)
        x = _matmul(oc, xa_w_o, i, res=x)

        hm = _rmsnorm(x, g_mlp, i, BF16)
        hid = _matmul(hm, mlp_w_up, i, mode="relu2", out_dtype=BF16)
        x = _matmul(hid, mlp_w_down, i, res=x, bm=1152, bn=1024, bk=1024)

    y = _rmsnorm(x, norm_final.reshape(1, 1, d), 0, F32)
    y_prompt = y[:mp].reshape(pb, pseq, d)
    y_sample = y[mp:].reshape(sb, sseq, d)
    shp5 = (DEPTH, -1, MEM_LEN, XA_HEADS, XA_HEAD_DIM)
    return (y_prompt, y_sample,
            jnp.stack(outs["pa_re"]), jnp.stack(outs["pa_im"]), jnp.stack(outs["pb"]),
            jnp.stack(outs["pbc"]), jnp.stack(outs["pc"]),
            p_mem_k.reshape(shp5), p_mem_v.reshape(shp5),
            jnp.stack(outs["sa_re"]), jnp.stack(outs["sa_im"]), jnp.stack(outs["sb"]),
            jnp.stack(outs["sbc"]), jnp.stack(outs["sc"]))
```

```python
import functools
import math

import jax
import jax.numpy as jnp
from jax import lax
from jax.experimental import pallas as pl
from jax.experimental.pallas import tpu as pltpu

F32 = jnp.float32
BF16 = jnp.bfloat16
EPS = 1e-5

D_MODEL = 4096
DEPTH = 4
LANES = 128
ROWS = 128
VMEM_LIMIT = 56 * 1024 * 1024

S5_GROUP = 16
S5_GROUPS = D_MODEL // S5_GROUP
S5_STATE = 64
S5_GPB = LANES // S5_GROUP
S5_BLOCKS = D_MODEL // LANES
S5_SW = S5_GPB * S5_STATE

SSD_INNER = 2 * D_MODEL
SSD_HEAD_DIM = 64
SSD_HEADS = SSD_INNER // SSD_HEAD_DIM
SSD_GROUPS = 8
SSD_HPG = SSD_HEADS // SSD_GROUPS
SSD_STATE = 128
SSD_CONV = 4
SSD_GN = SSD_GROUPS * SSD_STATE
SSD_CONV_DIM = SSD_INNER + 2 * SSD_GN
SSD_IN_DIM = SSD_INNER + SSD_CONV_DIM + SSD_HEADS
SSD_GW = SSD_INNER // SSD_GROUPS

GLA_HEADS = 8
GLA_DK = D_MODEL // 2 // GLA_HEADS
GLA_DV = D_MODEL // GLA_HEADS
GLA_KEY = GLA_HEADS * GLA_DK
GLA_VAL = GLA_HEADS * GLA_DV
GLA_RANK = 16
GLA_TAU = 16.0
GLA_MAIN = 2 * GLA_KEY + 2 * GLA_VAL
GLA_SUB = 16

MEM_LEN = 256
XA_HEADS = 4
XA_HEAD_DIM = 128
XA_DIM = XA_HEADS * XA_HEAD_DIM
MLP_HIDDEN = 4 * D_MODEL


def _cparams(n_axes, vmem=VMEM_LIMIT):
    return pltpu.CompilerParams(dimension_semantics=("arbitrary",) * n_axes,
                                vmem_limit_bytes=vmem)


def _sigmoid(x):
    return 1.0 / (1.0 + jnp.exp(-x))


def _softplus(x):
    return jnp.maximum(x, 0.0) + jnp.log1p(jnp.exp(-jnp.abs(x)))


def _split3(x):
    hi = x.astype(BF16)
    r1 = x - hi.astype(F32)
    mid = r1.astype(BF16)
    lo = (r1 - mid.astype(F32)).astype(BF16)
    return hi, mid, lo


def _dot(a, b):
    return jnp.dot(a, b, preferred_element_type=F32)


def _dot_nt(a, b):
    return lax.dot_general(a, b, (((1,), (1,)), ((), ())), preferred_element_type=F32)


def _dot3_left(m01, x):
    return sum(_dot(m01, p) for p in _split3(x))


def _dot3_right(x, m01):
    return sum(_dot(p, m01) for p in _split3(x))


def _shared_out(kernel_fn, n_in, out_buf):
    if out_buf is None:
        return kernel_fn, [], [], {}

    def body(*refs):
        return kernel_fn(*refs[:n_in], *refs[n_in + 1:])

    return body, [out_buf], [pl.BlockSpec(memory_space=pl.ANY)], {n_in: 0}


def _rmsnorm_kernel(x_ref, g_ref, o_ref):
    x = x_ref[...].astype(F32)
    ms = jnp.mean(x * x, axis=-1, keepdims=True)
    o_ref[...] = (x * lax.rsqrt(ms + EPS) * g_ref[...]).astype(o_ref.dtype)


def _rmsnorm(x, g_stack, layer, out_dtype, bm=256, row0=0, nrows=None):
    m, d = x.shape
    nrows = m if nrows is None else nrows
    assert row0 % bm == 0 and nrows % bm == 0
    blk0 = row0 // bm
    return pl.pallas_call(
        _rmsnorm_kernel,
        grid=(nrows // bm,),
        in_specs=[pl.BlockSpec((bm, d), lambda i: (blk0 + i, 0)),
                  pl.BlockSpec((None, 1, d), lambda i: (layer, 0, 0))],
        out_specs=pl.BlockSpec((bm, d), lambda i: (i, 0)),
        out_shape=jax.ShapeDtypeStruct((nrows, d), out_dtype),
        compiler_params=_cparams(1),
        name="rmsnorm",
    )(x, g_stack)


def _epilogue(mode, acc, acc2):
    if mode == "relu2":
        r = jnp.maximum(acc, 0.0)
        return r * r
    if mode == "glu":
        return acc * _sigmoid(acc2)
    return acc


def _mm_kernel_fullk(*refs, mode, has_res):
    it = iter(refs)
    x_ref = next(it)
    w_ref = next(it)
    w2_ref = next(it) if mode == "glu" else None
    res_ref = next(it) if has_res else None
    o_ref = next(it)
    wbf_ref = next(it)
    wbf2_ref = next(it) if mode == "glu" else None

    @pl.when(pl.program_id(1) == 0)
    def _():
        wbf_ref[...] = w_ref[...].astype(BF16)
        if mode == "glu":
            wbf2_ref[...] = w2_ref[...].astype(BF16)

    x = x_ref[...].astype(BF16)
    acc = _dot(x, wbf_ref[...])
    acc2 = _dot(x, wbf2_ref[...]) if mode == "glu" else None
    out = _epilogue(mode, acc, acc2)
    if has_res:
        out = out + res_ref[...]
    o_ref[...] = out.astype(o_ref.dtype)


def _mm_kernel_splitk(*refs, mode, has_res, nk):
    it = iter(refs)
    x_ref = next(it)
    w_ref = next(it)
    res_ref = next(it) if has_res else None
    o_ref = next(it)
    acc_ref = next(it)
    k = pl.program_id(2)

    @pl.when(k == 0)
    def _():
        acc_ref[...] = jnp.zeros_like(acc_ref)

    acc_ref[...] += _dot(x_ref[...].astype(BF16), w_ref[...].astype(BF16))

    @pl.when(k == nk - 1)
    def _():
        out = _epilogue(mode, acc_ref[...], None)
        if has_res:
            out = out + res_ref[...]
        o_ref[...] = out.astype(o_ref.dtype)


def _matmul(x, w, layer, *, n_out=None, w2=None, res=None, mode="plain", out_dtype=F32,
            bm=1024, bn=512, bk=None):
    m, kdim = x.shape
    n = n_out if n_out is not None else w.shape[2]
    bk = kdim if bk is None else bk
    nk = kdim // bk
    has_res = res is not None
    assert m % bm == 0 and n % bn == 0 and kdim % bk == 0
    if nk == 1:
        grid = (n // bn, m // bm)
        in_specs = [pl.BlockSpec((bm, kdim), lambda j, i: (i, 0)),
                    pl.BlockSpec((None, kdim, bn), lambda j, i: (layer, 0, j))]
        args = [x, w]
        scratch = [pltpu.VMEM((kdim, bn), BF16)]
        if mode == "glu":
            in_specs.append(pl.BlockSpec((None, kdim, bn), lambda j, i: (layer, 0, j)))
            args.append(w2)
            scratch.append(pltpu.VMEM((kdim, bn), BF16))
        if has_res:
            in_specs.append(pl.BlockSpec((bm, bn), lambda j, i: (i, j)))
            args.append(res)
        out_spec = pl.BlockSpec((bm, bn), lambda j, i: (i, j))
        body = functools.partial(_mm_kernel_fullk, mode=mode, has_res=has_res)
        n_axes = 2
    else:
        assert mode != "glu"
        grid = (n // bn, m // bm, nk)
        in_specs = [pl.BlockSpec((bm, bk), lambda j, i, k: (i, k)),
                    pl.BlockSpec((None, bk, bn), lambda j, i, k: (layer, k, j))]
        args = [x, w]
        if has_res:
            in_specs.append(pl.BlockSpec((bm, bn), lambda j, i, k: (i, j)))
            args.append(res)
        out_spec = pl.BlockSpec((bm, bn), lambda j, i, k: (i, j))
        scratch = [pltpu.VMEM((bm, bn), F32)]
        body = functools.partial(_mm_kernel_splitk, mode=mode, has_res=has_res, nk=nk)
        n_axes = 3
    return pl.pallas_call(
        body, grid=grid, in_specs=in_specs, out_specs=out_spec,
        out_shape=jax.ShapeDtypeStruct((m, n), out_dtype),
        scratch_shapes=scratch, compiler_params=_cparams(n_axes),
        name="matmul_" + mode,
    )(*args)


def _xattn_kernel(q_ref, k_ref, v_ref, o_ref, *, bb, lq):
    scale = XA_HEAD_DIM ** -0.5
    for ib in range(bb):
        rows = slice(ib * lq, (ib + 1) * lq)
        for h in range(XA_HEADS):
            sl = slice(h * XA_HEAD_DIM, (h + 1) * XA_HEAD_DIM)
            q = q_ref[rows, sl].astype(BF16)
            k = k_ref[ib, :, sl].astype(BF16)
            v = v_ref[ib, :, sl].astype(BF16)
            s = _dot_nt(q, k) * scale
            p = jnp.exp(s - jnp.max(s, axis=-1, keepdims=True))
            l = jnp.sum(p, axis=-1, keepdims=True)
            o_ref[rows, sl] = (_dot(p.astype(BF16), v) / l).astype(o_ref.dtype)


def _xattn(q, mk, mv, layer, row0, nseq, seq, *, bb, lq, out_buf=None):
    assert bb == 1 or lq == seq
    brows = bb * lq
    assert row0 % brows == 0
    nq = seq // lq
    blk0 = row0 // brows

    def qblk(i, j):
        return (blk0 + i * nq + j, 0)

    in_specs = [pl.BlockSpec((brows, XA_DIM), qblk),
                pl.BlockSpec((None, bb, MEM_LEN, XA_DIM), lambda i, j: (layer, i, 0, 0)),
                pl.BlockSpec((None, bb, MEM_LEN, XA_DIM), lambda i, j: (layer, i, 0, 0))]
    body, extra_args, extra_specs, aliases = _shared_out(
        functools.partial(_xattn_kernel, bb=bb, lq=lq), len(in_specs), out_buf)
    return pl.pallas_call(
        body,
        grid=(nseq // bb, nq),
        input_output_aliases=aliases,
        in_specs=in_specs + extra_specs,
        out_specs=pl.BlockSpec((brows, XA_DIM), qblk),
        out_shape=jax.ShapeDtypeStruct(q.shape, F32),
        compiler_params=_cparams(2),
        name="xattn",
    )(q, mk, mv, *extra_args)


def _s5_kernel(u_ref, h0r_ref, h0i_ref, ktr_ref, vc_ref, wc_ref, at_ref, d_ref,
               act_ref, hr_ref, hi_ref, ucat_ref, s_ref, hin_ref, ybuf_ref, bp_ref, cp_ref,
               *, nb, nc, t_len):
    r = nb * nc
    row_g = (lax.broadcasted_iota(jnp.int32, (t_len * LANES, 1), 0) // S5_GROUP) % S5_GPB
    col_h = lax.broadcasted_iota(jnp.int32, (1, S5_SW), 1) // S5_STATE
    for part in range(2):
        pp = vc_ref[0, :, part * LANES:(part + 1) * LANES]
        tiled = jnp.concatenate([pp] * (S5_SW // LANES), axis=1)
        bp_ref[:, part * S5_SW:(part + 1) * S5_SW] = jnp.where(row_g == col_h, tiled, 0.0).astype(BF16)
    row_h = lax.broadcasted_iota(jnp.int32, (S5_SW, 1), 0) // S5_STATE
    col_g = lax.broadcasted_iota(jnp.int32, (1, LANES), 1) // S5_GROUP
    for t in range(t_len):
        for part in range(2):
            piece = wc_ref[0, t, part * S5_STATE:(part + 1) * S5_STATE, :]
            tiled = jnp.concatenate([piece] * S5_GPB, axis=0)
            cp_ref[t, part * S5_SW:(part + 1) * S5_SW, :] = jnp.where(row_h == col_g, tiled, 0.0).astype(BF16)

    for s in range(t_len):
        ucat_ref[:, s * LANES:(s + 1) * LANES] = u_ref[pl.ds(s, r, stride=t_len), :].astype(BF16)
    half = S5_SW // LANES
    s_all = _dot(ucat_ref[...], bp_ref[...])
    for k in range(2 * half):
        s_ref[k] = s_all[:, k * LANES:(k + 1) * LANES]
    a_sl = [at_ref[0, :, k * LANES:(k + 1) * LANES] for k in range(2 * half)]

    def body(c, carry):
        rows_c = pl.ds(c, nb, stride=nc)
        new = []
        for k in range(half):
            hr, hi = carry[k], carry[half + k]
            hin_ref[k, rows_c, :] = hr
            hin_ref[half + k, rows_c, :] = hi
            new.append((a_sl[k] * hr - a_sl[half + k] * hi + s_ref[k, rows_c, :],
                        a_sl[k] * hi + a_sl[half + k] * hr + s_ref[half + k, rows_c, :]))
        return tuple(n[0] for n in new) + tuple(n[1] for n in new)

    h0 = (tuple(h0r_ref[:, k * LANES:(k + 1) * LANES] for k in range(half))
          + tuple(h0i_ref[:, k * LANES:(k + 1) * LANES] for k in range(half)))
    hfin = lax.fori_loop(0, nc, body, h0)
    for k in range(half):
        hr_ref[:, k * LANES:(k + 1) * LANES] = hfin[k]
        hi_ref[:, k * LANES:(k + 1) * LANES] = hfin[half + k]
    hin = jnp.concatenate([hin_ref[k] for k in range(2 * half)], axis=1).astype(BF16)
    for t in range(t_len):
        y = _dot(hin, cp_ref[t])
        y = y + _dot(ucat_ref[:, :(t + 1) * LANES], ktr_ref[0, (t_len - 1 - t) * LANES:, :])
        ut = u_ref[pl.ds(t, r, stride=t_len), :]
        y = y + d_ref[...] * ut
        ybuf_ref[pl.ds(t, r, stride=t_len), :] = jax.nn.gelu(y)
    act_ref[...] = ybuf_ref[...].astype(act_ref.dtype)


def _s5_prep(a_re, a_im, log_dt, b_re, b_im, c_re, c_im, t_len):
    hp = lax.Precision.HIGHEST
    g, p = a_re.shape
    lam_re = a_re.astype(F32)
    lam_im = a_im.astype(F32)
    dt = jnp.exp(log_dt.astype(F32))[:, None]
    kk = jnp.arange(t_len + 1, dtype=F32)[:, None, None]
    magk = jnp.exp(kk * (lam_re * dt))
    angk = kk * (lam_im * dt)
    pw_re = magk * jnp.cos(angk)
    pw_im = magk * jnp.sin(angk)
    ab_re, ab_im = pw_re[1], pw_im[1]
    den = lam_re * lam_re + lam_im * lam_im
    f_re = ((ab_re - 1.0) * lam_re + ab_im * lam_im) / den
    f_im = (ab_im * lam_re - (ab_re - 1.0) * lam_im) / den
    br = b_re.astype(F32)
    bi = b_im.astype(F32)
    bb_re = f_re[..., None] * br - f_im[..., None] * bi
    bb_im = f_re[..., None] * bi + f_im[..., None] * br
    cr = c_re.astype(F32)
    ci = c_im.astype(F32)
    eye = jnp.eye(S5_GPB, dtype=F32)
    nblk = g // S5_GPB

    kr = jnp.float32(t_len - 1) - jnp.arange(t_len, dtype=F32)[:, None, None]
    magr = jnp.exp(kr * (lam_re * dt))
    angr = kr * (lam_im * dt)
    rev = magr * jnp.cos(angr), magr * jnp.sin(angr)
    v_re = rev[0][..., None] * bb_re - rev[1][..., None] * bb_im
    v_im = rev[0][..., None] * bb_im + rev[1][..., None] * bb_re
    vc = jnp.stack([v_re, v_im], axis=0)
    vc = vc.reshape(2, t_len, nblk, S5_GPB, p, S5_GROUP)
    vt = jnp.transpose(vc, (2, 1, 3, 5, 0, 4))
    bp = jnp.broadcast_to(vt[:, :, :, :, :, None, :], (nblk, t_len, S5_GPB, S5_GROUP, 2, 2, p))
    bp = bp.reshape(nblk, t_len * LANES, 2 * LANES)

    w_re = cr[None] * pw_re[1:, :, None, :] - ci[None] * pw_im[1:, :, None, :]
    w_im = cr[None] * pw_im[1:, :, None, :] + ci[None] * pw_re[1:, :, None, :]
    wc = jnp.stack([w_re, -w_im], axis=0).reshape(2, t_len, nblk, S5_GPB, S5_GROUP, p)
    wt = jnp.transpose(wc, (2, 1, 0, 5, 3, 4))
    cp = wt.reshape(nblk, t_len, 2 * p, LANES)

    kt = (jnp.einsum('gcp,tgpd->tgcd', cr, v_re, precision=hp)
          - jnp.einsum('gcp,tgpd->tgcd', ci, v_im, precision=hp))
    kt = kt.reshape(t_len, nblk, S5_GPB, S5_GROUP, S5_GROUP)
    ktt = jnp.transpose(kt, (1, 0, 2, 4, 3))
    ktr = (ktt[:, :, :, :, None, :] * eye.reshape(1, 1, S5_GPB, 1, S5_GPB, 1)).astype(BF16)
    ktr = ktr.reshape(nblk, t_len * LANES, LANES)

    at = jnp.concatenate([pw_re[t_len].reshape(nblk, 1, S5_SW),
                          pw_im[t_len].reshape(nblk, 1, S5_SW)], axis=2)
    return ktr, bp, cp, at


def _s5_core(u, row0, nb, seq, t_len, h0_re, h0_im, ops, d_skip, out_buf=None):
    ktr, bp, cp, at = ops
    nc = seq // t_len
    r = nb * nc
    nrows = nb * seq
    assert row0 % nrows == 0
    rb = row0 // nrows
    h0r = h0_re.reshape(nb, S5_GROUPS * S5_STATE)
    h0i = h0_im.reshape(nb, S5_GROUPS * S5_STATE)
    out_shapes = (jax.ShapeDtypeStruct((u.shape[0], D_MODEL), BF16),
                  jax.ShapeDtypeStruct((nb, S5_GROUPS * S5_STATE), F32),
                  jax.ShapeDtypeStruct((nb, S5_GROUPS * S5_STATE), F32))
    body, extra_args, extra_specs, aliases = _shared_out(
        functools.partial(_s5_kernel, nb=nb, nc=nc, t_len=t_len), 8, out_buf)
    act, hr, hi = pl.pallas_call(
        body,
        grid=(S5_BLOCKS,),
        input_output_aliases=aliases,
        in_specs=[pl.BlockSpec((nrows, LANES), lambda j: (rb, j)),
                  pl.BlockSpec((nb, S5_SW), lambda j: (0, j)),
                  pl.BlockSpec((nb, S5_SW), lambda j: (0, j)),
                  pl.BlockSpec((1, t_len * LANES, LANES), lambda j: (j, 0, 0)),
                  pl.BlockSpec((1, t_len * LANES, 2 * LANES), lambda j: (j, 0, 0)),
                  pl.BlockSpec((1, t_len, 2 * S5_STATE, LANES), lambda j: (j, 0, 0, 0)),
                  pl.BlockSpec((1, 1, 2 * S5_SW), lambda j: (j, 0, 0)),
                  pl.BlockSpec((1, LANES), lambda j: (0, j))] + extra_specs,
        out_specs=(pl.BlockSpec((nrows, LANES), lambda j: (rb, j)),
                   pl.BlockSpec((nb, S5_SW), lambda j: (0, j)),
                   pl.BlockSpec((nb, S5_SW), lambda j: (0, j))),
        out_shape=out_shapes,
        scratch_shapes=[pltpu.VMEM((r, t_len * LANES), BF16),
                        pltpu.VMEM((2 * S5_SW // LANES, r, LANES), F32),
                        pltpu.VMEM((2 * S5_SW // LANES, r, LANES), F32),
                        pltpu.VMEM((nrows, LANES), F32),
                        pltpu.VMEM((t_len * LANES, 2 * S5_SW), BF16),
                        pltpu.VMEM((t_len, 2 * S5_SW, LANES), BF16)],
        compiler_params=_cparams(1),
        name="s5_core",
    )(u, h0r, h0i, ktr, bp, cp, at, d_skip.reshape(1, D_MODEL), *extra_args)
    return (act, hr.reshape(nb, S5_GROUPS, S5_STATE), hi.reshape(nb, S5_GROUPS, S5_STATE))


def _ssd_kernel(z_ref, x_ref, b_ref, c_ref, dt_ref, dtt_ref,
                cwx_ref, cwb_ref, cwc_ref, cbx_ref, cbb_ref, cbc_ref,
                csx_ref, csb_ref, csc_ref,
                dtb_ref, dtbt_ref, alog_ref, alogt_ref, dexp_ref, norm_ref, st0_ref,
                y_ref, st_ref, xbuf, bbuf, cbuf, *, nb, qt):
    rows = nb * qt
    c = pl.program_id(2)
    pad = 8
    hist = SSD_CONV - 1

    @pl.when(c == 0)
    def _():
        st_ref[...] = st0_ref[...]
        xbuf[:, pad - hist:pad, :] = csx_ref[...]
        bbuf[:, pad - hist:pad, :] = csb_ref[...]
        cbuf[:, pad - hist:pad, :] = csc_ref[...]

    def conv(buf, raw_ref, w_ref, bias_ref, width):
        buf[:, pad:, :] = raw_ref[...].reshape(nb, qt, width)
        acc = bias_ref[...].reshape(1, 1, width)
        for k in range(SSD_CONV):
            acc = acc + buf[:, pad - hist + k:pad - hist + k + qt, :] * w_ref[k:k + 1, :].reshape(1, 1, width)
        tail = buf[:, pad + qt - hist:pad + qt, :]
        buf[:, pad - hist:pad, :] = tail
        acc = acc * _sigmoid(acc)
        return acc.reshape(rows, width)

    xs = conv(xbuf, x_ref, cwx_ref, cbx_ref, SSD_GW)
    bs = conv(bbuf, b_ref, cwb_ref, cbb_ref, SSD_STATE).astype(BF16)
    cs = conv(cbuf, c_ref, cwc_ref, cbc_ref, SSD_STATE).astype(BF16)

    dt = _softplus(dt_ref[0] + dtb_ref[0])
    dtt = _softplus(dtt_ref[0] + dtbt_ref[0])
    da = dt * (-jnp.exp(alog_ref[0]))
    dat = dtt * (-jnp.exp(alogt_ref[0]))

    ri = lax.broadcasted_iota(jnp.int32, (rows, rows), 0)
    ci = lax.broadcasted_iota(jnp.int32, (rows, rows), 1)
    same = (ri // qt) == (ci // qt)
    tri = same & (ri >= ci)
    tri_bf = tri.astype(BF16)
    trit_bf = (same & (ri <= ci)).astype(BF16)
    same_bf = same.astype(BF16)
    cum = _dot3_left(tri_bf, da)
    cumt = _dot3_right(dat, trit_bf)
    tot = _dot3_left(same_bf, da)
    cb = _dot_nt(cs, bs)

    lane = lax.broadcasted_iota(jnp.int32, (1, LANES), 1)
    lo = lane < SSD_HEAD_DIM
    srow = lax.broadcasted_iota(jnp.int32, (LANES, 1), 0) < SSD_HEAD_DIM
    row_seq = lax.broadcasted_iota(jnp.int32, (rows, 1), 0) // qt
    neg_inf = jnp.float32(-jnp.inf)
    bs_seq = [bs if nb == 1 else jnp.where(row_seq == ib, bs, jnp.zeros_like(bs)) for ib in range(nb)]

    ys = []
    for p in range(SSD_HPG // 2):
        ra, rb = 2 * p, 2 * p + 1
        ws = []
        for r in (ra, rb):
            seg = cum[:, r:r + 1] - cumt[r:r + 1, :]
            dec = jnp.exp(jnp.where(tri, seg, neg_inf))
            ws.append((cb * dec * dtt[r:r + 1, :]).astype(BF16))
        xp = xs[:, p * LANES:(p + 1) * LANES]
        y = (_dot(ws[0], jnp.where(lo, xp, 0.0).astype(BF16))
             + _dot(ws[1], jnp.where(lo, 0.0, xp).astype(BF16)))
        ecol = jnp.where(lo, jnp.exp(cum[:, ra:ra + 1]), jnp.exp(cum[:, rb:rb + 1]))
        inter = []
        for ib in range(nb):
            st = st_ref[ib, 0, p * LANES:(p + 1) * LANES, :].astype(BF16)
            inter.append(_dot_nt(cs[ib * qt:(ib + 1) * qt, :], st))
        inter = inter[0] if nb == 1 else jnp.concatenate(inter, axis=0)
        y = y + inter * ecol + dexp_ref[:, p * LANES:(p + 1) * LANES] * xp
        ys.append(y)

        sc = jnp.where(lo, jnp.exp(tot[:, ra:ra + 1] - cum[:, ra:ra + 1]) * dt[:, ra:ra + 1],
                       jnp.exp(tot[:, rb:rb + 1] - cum[:, rb:rb + 1]) * dt[:, rb:rb + 1])
        xwt = jnp.transpose(xp * sc).astype(BF16)
        for ib in range(nb):
            upd = _dot(xwt, bs_seq[ib])
            r0 = ib * qt
            dcol = jnp.where(srow, jnp.exp(tot[r0:r0 + 1, ra:ra + 1]), jnp.exp(tot[r0:r0 + 1, rb:rb + 1]))
            old = st_ref[ib, 0, p * LANES:(p + 1) * LANES, :]
            st_ref[ib, 0, p * LANES:(p + 1) * LANES, :] = old * dcol + upd

    y = jnp.concatenate(ys, axis=1)
    zg = z_ref[...]
    y = y * (zg * _sigmoid(zg))
    ms = jnp.mean(y * y, axis=-1, keepdims=True)
    y_ref[...] = (y * lax.rsqrt(ms + EPS) * norm_ref[...]).astype(y_ref.dtype)


def _ssd_core(proj, dtp, dtt, row_blk0, nseq, nb, qt, nc, st0, conv0, params, out_buf=None):
    cw, cbias, dtb, dtbt, alog, alogt, dexp, norm = params
    nbb = nseq // nb
    xoff = SSD_INNER // SSD_GW
    boff = (SSD_INNER + SSD_INNER) // SSD_STATE
    coff = boff + SSD_GN // SSD_STATE
    cwb_off = SSD_INNER // SSD_STATE
    cwc_off = cwb_off + SSD_GN // SSD_STATE

    def rblk(i, g, c):
        return row_blk0 + i * nc + c

    st0 = st0.reshape(nseq, SSD_GROUPS, SSD_GW, SSD_STATE)
    hist = SSD_CONV - 1
    in_specs = [
        pl.BlockSpec((ROWS, SSD_GW), lambda i, g, c: (rblk(i, g, c), g)),
        pl.BlockSpec((ROWS, SSD_GW), lambda i, g, c: (rblk(i, g, c), xoff + g)),
        pl.BlockSpec((ROWS, SSD_STATE), lambda i, g, c: (rblk(i, g, c), boff + g)),
        pl.BlockSpec((ROWS, SSD_STATE), lambda i, g, c: (rblk(i, g, c), coff + g)),
        pl.BlockSpec((1, ROWS, LANES), lambda i, g, c: (g, rblk(i, g, c), 0)),
        pl.BlockSpec((1, SSD_HPG, ROWS), lambda i, g, c: (g, 0, rblk(i, g, c))),
        pl.BlockSpec((SSD_CONV, SSD_GW), lambda i, g, c: (0, g)),
        pl.BlockSpec((SSD_CONV, SSD_STATE), lambda i, g, c: (0, cwb_off + g)),
        pl.BlockSpec((SSD_CONV, SSD_STATE), lambda i, g, c: (0, cwc_off + g)),
        pl.BlockSpec((1, SSD_GW), lambda i, g, c: (0, g)),
        pl.BlockSpec((1, SSD_STATE), lambda i, g, c: (0, cwb_off + g)),
        pl.BlockSpec((1, SSD_STATE), lambda i, g, c: (0, cwc_off + g)),
        pl.BlockSpec((nb, hist, SSD_GW), lambda i, g, c: (i, 0, g)),
        pl.BlockSpec((nb, hist, SSD_STATE), lambda i, g, c: (i, 0, cwb_off + g)),
        pl.BlockSpec((nb, hist, SSD_STATE), lambda i, g, c: (i, 0, cwc_off + g)),
        pl.BlockSpec((1, 1, LANES), lambda i, g, c: (g, 0, 0)),
        pl.BlockSpec((1, SSD_HPG, 1), lambda i, g, c: (g, 0, 0)),
        pl.BlockSpec((1, 1, LANES), lambda i, g, c: (g, 0, 0)),
        pl.BlockSpec((1, SSD_HPG, 1), lambda i, g, c: (g, 0, 0)),
        pl.BlockSpec((1, SSD_GW), lambda i, g, c: (0, g)),
        pl.BlockSpec((1, SSD_GW), lambda i, g, c: (0, g)),
        pl.BlockSpec((nb, 1, SSD_GW, SSD_STATE), lambda i, g, c: (i, g, 0, 0)),
    ]
    out_specs = (pl.BlockSpec((ROWS, SSD_GW), lambda i, g, c: (rblk(i, g, c), g)),
                 pl.BlockSpec((nb, 1, SSD_GW, SSD_STATE), lambda i, g, c: (i, g, 0, 0)))
    out_shape = (jax.ShapeDtypeStruct((proj.shape[0], SSD_INNER), BF16),
                 jax.ShapeDtypeStruct((nseq, SSD_GROUPS, SSD_GW, SSD_STATE), F32))
    body, extra_args, extra_specs, aliases = _shared_out(
        functools.partial(_ssd_kernel, nb=nb, qt=qt), len(in_specs), out_buf)
    y, st = pl.pallas_call(
        body,
        grid=(nbb, SSD_GROUPS, nc),
        input_output_aliases=aliases,
        in_specs=in_specs + extra_specs, out_specs=out_specs, out_shape=out_shape,
        scratch_shapes=[pltpu.VMEM((nb, 8 + qt, SSD_GW), F32),
                        pltpu.VMEM((nb, 8 + qt, SSD_STATE), F32),
                        pltpu.VMEM((nb, 8 + qt, SSD_STATE), F32)],
        compiler_params=_cparams(3),
        name="ssd_core",
    )(proj, proj, proj, proj, dtp, dtt, cw, cw, cw, cbias, cbias, cbias,
      conv0, conv0, conv0, dtb, dtbt, alog, alogt, dexp, norm, st0, *extra_args)
    return y, st.reshape(nseq, SSD_HEADS, SSD_HEAD_DIM, SSD_STATE)


def _gla_kernel(q_ref, k_ref, v_ref, r_ref, lr_ref, wa2_ref, ba_ref, norm_ref, s0_ref,
                o_ref, s_ref, cum_s, q_s, k_s, att_s, *, nb, qt, wq):
    rows = nb * qt
    c = pl.program_id(2)

    @pl.when(c == 0)
    def _():
        s_ref[...] = s0_ref[...]

    qv = q_ref[...] * (GLA_DK ** -0.5)
    kv = k_ref[...]
    v_bf = v_ref[...].astype(BF16)
    gate = _dot(lr_ref[...].astype(BF16), wa2_ref[...].astype(BF16)) + ba_ref[...]
    g = -_softplus(-gate) / GLA_TAU

    ri = lax.broadcasted_iota(jnp.int32, (rows, rows), 0)
    ci = lax.broadcasted_iota(jnp.int32, (rows, rows), 1)
    same = (ri // qt) == (ci // qt)
    tri_bf = (same & (ri >= ci)).astype(BF16)
    same_bf = same.astype(BF16)
    cum = _dot3_left(tri_bf, g)
    tot = _dot3_left(same_bf, g)
    cum_s[...] = cum
    q_s[...] = qv
    k_s[...] = kv

    qe = (qv * jnp.exp(cum)).astype(BF16)
    o_parts = [_dot(qe[ib * qt:(ib + 1) * qt, :], s_ref[ib, 0].astype(BF16)) for ib in range(nb)]
    o = o_parts[0] if nb == 1 else jnp.concatenate(o_parts, axis=0)

    neg_inf = jnp.float32(-jnp.inf)
    att_s[...] = jnp.zeros_like(att_s)
    if qt > wq:
        assert nb == 1
        rowi = lax.broadcasted_iota(jnp.int32, (rows, 1), 0)
        for blk in range(1, qt // wq):
            lo_r, hi_r = blk * wq, (blk + 1) * wq
            c0 = cum[lo_r - 1:lo_r, :]
            q_blk = (qv[lo_r:hi_r, :] * jnp.exp(cum[lo_r:hi_r, :] - c0)).astype(BF16)
            k_all = (kv * jnp.exp(jnp.where(rowi < lo_r, c0 - cum, neg_inf))).astype(BF16)
            att_s[lo_r:hi_r, :] = _dot_nt(q_blk, k_all)

    lane = lax.broadcasted_iota(jnp.int32, (1, rows), 1)
    wrow = lax.broadcasted_iota(jnp.int32, (wq, 1), 0)

    def window(w, carry):
        w0 = pl.multiple_of(w * wq, wq)
        cw = cum_s[pl.ds(w0, wq), :]
        qw = q_s[pl.ds(w0, wq), :]
        kw = k_s[pl.ds(w0, wq), :]
        blk = att_s[pl.ds(w0, wq), :]
        for jj in range(wq):
            e = jnp.exp(jnp.where(wrow >= jj, cw - cw[jj:jj + 1, :], neg_inf))
            col = jnp.sum(qw * e * kw[jj:jj + 1, :], axis=-1, keepdims=True)
            blk = jnp.where(lane == w0 + jj, col, blk)
        att_s[pl.ds(w0, wq), :] = blk
        return carry

    lax.fori_loop(0, rows // wq, window, 0)
    o = o + _dot(att_s[...].astype(BF16), v_bf)

    kdt = jnp.transpose(kv * jnp.exp(tot - cum)).astype(BF16)
    gt = jnp.transpose(g)
    rsel = lax.broadcasted_iota(jnp.int32, (rows, LANES), 0) // qt
    row_seq = lax.broadcasted_iota(jnp.int32, (rows, 1), 0) // qt
    for ib in range(nb):
        sel = (rsel == ib).astype(BF16)
        dec = jnp.exp(_dot3_right(gt, sel))
        dec = jnp.concatenate([dec] * (GLA_DV // LANES), axis=1)
        v_ib = v_bf if nb == 1 else jnp.where(row_seq == ib, v_bf, jnp.zeros_like(v_bf))
        s_ref[ib, 0] = s_ref[ib, 0] * dec + _dot(kdt, v_ib)

    ms = jnp.mean(o * o, axis=-1, keepdims=True)
    o = o * lax.rsqrt(ms + EPS) * norm_ref[...]
    rr = r_ref[...]
    o_ref[...] = (o * (rr * _sigmoid(rr))).astype(o_ref.dtype)


def _gla_core(proj, lrp, row_blk0, nseq, nb, qt, nc, wq, s0, params, out_buf=None):
    wa2p, ba, norm = params
    nbb = nseq // nb
    koff = GLA_KEY // GLA_DK
    voff = 2 * GLA_KEY // GLA_DV
    roff = voff + GLA_VAL // GLA_DV

    def rblk(i, h, c):
        return row_blk0 + i * nc + c

    in_specs = [
        pl.BlockSpec((ROWS, GLA_DK), lambda i, h, c: (rblk(i, h, c), h)),
        pl.BlockSpec((ROWS, GLA_DK), lambda i, h, c: (rblk(i, h, c), koff + h)),
        pl.BlockSpec((ROWS, GLA_DV), lambda i, h, c: (rblk(i, h, c), voff + h)),
        pl.BlockSpec((ROWS, GLA_DV), lambda i, h, c: (rblk(i, h, c), roff + h)),
        pl.BlockSpec((ROWS, LANES), lambda i, h, c: (rblk(i, h, c), 0)),
        pl.BlockSpec((LANES, GLA_DK), lambda i, h, c: (0, h)),
        pl.BlockSpec((1, GLA_DK), lambda i, h, c: (0, h)),
        pl.BlockSpec((1, GLA_DV), lambda i, h, c: (0, 0)),
        pl.BlockSpec((nb, 1, GLA_DK, GLA_DV), lambda i, h, c: (i, h, 0, 0)),
    ]
    out_specs = (pl.BlockSpec((ROWS, GLA_DV), lambda i, h, c: (rblk(i, h, c), h)),
                 pl.BlockSpec((nb, 1, GLA_DK, GLA_DV), lambda i, h, c: (i, h, 0, 0)))
    out_shape = (jax.ShapeDtypeStruct((proj.shape[0], GLA_VAL), BF16),
                 jax.ShapeDtypeStruct((nseq, GLA_HEADS, GLA_DK, GLA_DV), F32))
    body, extra_args, extra_specs, aliases = _shared_out(
        functools.partial(_gla_kernel, nb=nb, qt=qt, wq=wq), len(in_specs), out_buf)
    return pl.pallas_call(
        body,
        grid=(nbb, GLA_HEADS, nc),
        input_output_aliases=aliases,
        in_specs=in_specs + extra_specs, out_specs=out_specs, out_shape=out_shape,
        scratch_shapes=[pltpu.VMEM((ROWS, GLA_DK), F32), pltpu.VMEM((ROWS, GLA_DK), F32),
                        pltpu.VMEM((ROWS, GLA_DK), F32), pltpu.VMEM((ROWS, ROWS), F32)],
        compiler_params=_cparams(3),
        name="gla_core",
    )(proj, proj, proj, proj, lrp, wa2p, ba, norm, s0, *extra_args)


def kernel(x_prompt, x_sample, state_s5_re, state_s5_im, state_ssd, state_ssd_conv, state_gla, cache_mem_k, cache_mem_v, mem_prompt, norm_mix, norm_xattn, norm_mem, norm_mlp, norm_final, xa_w_q, xa_w_k, xa_w_v, xa_w_o, mlp_w_up, mlp_w_down, s5_a_re, s5_a_im, s5_log_dt, s5_b_re, s5_b_im, s5_c_re, s5_c_im, s5_d, s5_glu_w1, s5_glu_w2, ssd_w_in, ssd_conv_w, ssd_conv_b, ssd_dt_bias, ssd_a_log, ssd_d, ssd_norm, ssd_w_out, gla_w_in, gla_w_a2, gla_b_a, gla_norm, gla_w_out):
    pb, pseq, d = x_prompt.shape
    sb, sseq, _ = x_sample.shape
    mp = pb * pseq
    ms = sb * sseq
    assert mp % ROWS == 0 and ms % ROWS == 0 and pseq % ROWS == 0 and ROWS % sseq == 0
    x = jnp.concatenate([x_prompt.reshape(mp, d), x_sample.reshape(ms, d)], axis=0)
    m = mp + ms
    p_blk0 = 0
    s_blk0 = mp // ROWS
    s_nb = ROWS // sseq

    g_mix = norm_mix.reshape(DEPTH, 1, d)
    g_xa = norm_xattn.reshape(DEPTH, 1, d)
    g_mem = norm_mem.reshape(DEPTH, 1, d)
    g_mlp = norm_mlp.reshape(DEPTH, 1, d)

    mem2 = mem_prompt.reshape(pb * MEM_LEN, d)
    pk, pv = [], []
    for i in range(DEPTH):
        mn = _rmsnorm(mem2, g_mem, i, BF16)
        pk.append(_matmul(mn, xa_w_k, i, bm=512, bn=XA_DIM))
        pv.append(_matmul(mn, xa_w_v, i, bm=512, bn=XA_DIM))
    p_mem_k = jnp.stack(pk).reshape(DEPTH, pb, MEM_LEN, XA_DIM)
    p_mem_v = jnp.stack(pv).reshape(DEPTH, pb, MEM_LEN, XA_DIM)
    c_mem_k = cache_mem_k.reshape(DEPTH, sb, MEM_LEN, XA_DIM)
    c_mem_v = cache_mem_v.reshape(DEPTH, sb, MEM_LEN, XA_DIM)

    outs = {k: [] for k in ("pa_re", "pa_im", "pb", "pbc", "pc", "sa_re", "sa_im", "sb", "sbc", "sc")}
    ia = ib = ic = 0
    for i in range(DEPTH):
        kind = i % 3
        if kind == 0:
            hn = _rmsnorm(x, g_mix, i, F32)
            ops_p = _s5_prep(s5_a_re[ia], s5_a_im[ia], s5_log_dt[ia], s5_b_re[ia], s5_b_im[ia],
                             s5_c_re[ia], s5_c_im[ia], 16)
            ops_s = _s5_prep(s5_a_re[ia], s5_a_im[ia], s5_log_dt[ia], s5_b_re[ia], s5_b_im[ia],
                             s5_c_re[ia], s5_c_im[ia], sseq)
            zero = jnp.zeros((pb, S5_GROUPS, S5_STATE), F32)
            act, hr_p, hi_p = _s5_core(hn, 0, pb, pseq, 16, zero, zero, ops_p, s5_d[ia])
            act, hr_s, hi_s = _s5_core(hn, mp, sb, sseq, sseq, state_s5_re[ia], state_s5_im[ia],
                                       ops_s, s5_d[ia], out_buf=act)
            outs["pa_re"].append(hr_p)
            outs["pa_im"].append(hi_p)
            outs["sa_re"].append(hr_s)
            outs["sa_im"].append(hi_s)
            x = _matmul(act, s5_glu_w1, ia, w2=s5_glu_w2, res=x, mode="glu", bn=256)
            ia += 1
        elif kind == 1:
            hn = _rmsnorm(x, g_mix, i, BF16)
            proj = _matmul(hn, ssd_w_in, ib, bn=640)
            dt_raw = proj[:, SSD_INNER + SSD_CONV_DIM:].reshape(m, SSD_GROUPS, SSD_HPG)
            dtt = jnp.transpose(dt_raw, (1, 2, 0))
            dtp = jnp.pad(jnp.transpose(dt_raw, (1, 0, 2)), ((0, 0), (0, 0), (0, LANES - SSD_HPG)))
            pad_h = ((0, 0), (0, 0), (0, LANES - SSD_HPG))
            params = (ssd_conv_w[ib], ssd_conv_b[ib].reshape(1, SSD_CONV_DIM),
                      jnp.pad(ssd_dt_bias[ib].reshape(SSD_GROUPS, 1, SSD_HPG), pad_h),
                      ssd_dt_bias[ib].reshape(SSD_GROUPS, SSD_HPG, 1),
                      jnp.pad(ssd_a_log[ib].reshape(SSD_GROUPS, 1, SSD_HPG), pad_h),
                      ssd_a_log[ib].reshape(SSD_GROUPS, SSD_HPG, 1),
                      jnp.repeat(ssd_d[ib], SSD_HEAD_DIM).reshape(1, SSD_INNER),
                      ssd_norm[ib].reshape(1, SSD_INNER))
            yb, st_p = _ssd_core(proj, dtp, dtt, p_blk0, pb, 1, ROWS, pseq // ROWS,
                                  jnp.zeros((pb, SSD_HEADS, SSD_HEAD_DIM, SSD_STATE), F32),
                                  jnp.zeros((pb, SSD_CONV - 1, SSD_CONV_DIM), F32), params)
            yb, st_s = _ssd_core(proj, dtp, dtt, s_blk0, sb, s_nb, sseq, 1,
                                 state_ssd[ib], state_ssd_conv[ib], params, out_buf=yb)
            xbc = proj[:, SSD_INNER:SSD_INNER + SSD_CONV_DIM]
            outs["pb"].append(st_p)
            outs["sb"].append(st_s)
            outs["pbc"].append(xbc[:mp].reshape(pb, pseq, SSD_CONV_DIM)[:, pseq - (SSD_CONV - 1):])
            outs["sbc"].append(xbc[mp:].reshape(sb, sseq, SSD_CONV_DIM)[:, sseq - (SSD_CONV - 1):])
            x = _matmul(yb, ssd_w_out, ib, res=x, bm=1152, bn=1024, bk=1024)
            ib += 1
        else:
            hn = _rmsnorm(x, g_mix, i, BF16)
            proj = _matmul(hn, gla_w_in, ic, n_out=GLA_MAIN)
            w_lr = jnp.pad(gla_w_in[ic][:, GLA_MAIN:], ((0, 0), (0, LANES - GLA_RANK)))[None]
            lrp = _matmul(hn, w_lr, 0, bn=LANES)
            params = (jnp.pad(gla_w_a2[ic], ((0, LANES - GLA_RANK), (0, 0))),
                      gla_b_a[ic].reshape(1, GLA_KEY), gla_norm[ic].reshape(1, GLA_DV))
            ob, s_p = _gla_core(proj, lrp, p_blk0, pb, 1, ROWS, pseq // ROWS, GLA_SUB,
                                 jnp.zeros((pb, GLA_HEADS, GLA_DK, GLA_DV), F32), params)
            ob, s_s = _gla_core(proj, lrp, s_blk0, sb, s_nb, sseq, 1, sseq, state_gla[ic], params,
                                out_buf=ob)
            outs["pc"].append(s_p)
            outs["sc"].append(s_s)
            x = _matmul(ob, gla_w_out, ic, res=x)
            ic += 1

        hx = _rmsnorm(x, g_xa, i, BF16)
        qp = _matmul(hx, xa_w_q, i, bn=XA_DIM)
        oc = _xattn(qp, p_mem_k, p_mem_v, i, 0, pb, pseq, bb=1, lq=512)
        oc = _xattn(qp, c_mem_k, c_mem_v, i, mp, sb, sseq, bb=8, lq=sseq, out_buf=oc)
        x = _matmul(oc, xa_w_o, i, res=x)

        hm = _rmsnorm(x, g_mlp, i, BF16)
        hid = _matmul(hm, mlp_w_up, i, mode="relu2", out_dtype=BF16)
        x = _matmul(hid, mlp_w_down, i, res=x, bm=1152, bn=1024, bk=1024)

    g_fin = norm_final.reshape(1, 1, d)
    y_prompt = _rmsnorm(x, g_fin, 0, F32, row0=0, nrows=mp).reshape(pb, pseq, d)
    y_sample = _rmsnorm(x, g_fin, 0, F32, row0=mp, nrows=ms).reshape(sb, sseq, d)
    shp5 = (DEPTH, -1, MEM_LEN, XA_HEADS, XA_HEAD_DIM)
    return (y_prompt, y_sample,
            jnp.stack(outs["pa_re"]), jnp.stack(outs["pa_im"]), jnp.stack(outs["pb"]),
            jnp.stack(outs["pbc"]), jnp.stack(outs["pc"]),
            p_mem_k.reshape(shp5), p_mem_v.reshape(shp5),
            jnp.stack(outs["sa_re"]), jnp.stack(outs["sa_im"]), jnp.stack(outs["sb"]),
            jnp.stack(outs["sbc"]), jnp.stack(outs["sc"]))
```

```python
import functools
import math

import jax
import jax.numpy as jnp
from jax import lax
from jax.experimental import pallas as pl
from jax.experimental.pallas import tpu as pltpu

F32 = jnp.float32
BF16 = jnp.bfloat16
EPS = 1e-5

D_MODEL = 4096
DEPTH = 4
LANES = 128
ROWS = 128
VMEM_LIMIT = 56 * 1024 * 1024

S5_GROUP = 16
S5_GROUPS = D_MODEL // S5_GROUP
S5_STATE = 64
S5_GPB = LANES // S5_GROUP
S5_BLOCKS = D_MODEL // LANES
S5_SW = S5_GPB * S5_STATE

SSD_INNER = 2 * D_MODEL
SSD_HEAD_DIM = 64
SSD_HEADS = SSD_INNER // SSD_HEAD_DIM
SSD_GROUPS = 8
SSD_HPG = SSD_HEADS // SSD_GROUPS
SSD_STATE = 128
SSD_CONV = 4
SSD_GN = SSD_GROUPS * SSD_STATE
SSD_CONV_DIM = SSD_INNER + 2 * SSD_GN
SSD_IN_DIM = SSD_INNER + SSD_CONV_DIM + SSD_HEADS
SSD_GW = SSD_INNER // SSD_GROUPS

GLA_HEADS = 8
GLA_DK = D_MODEL // 2 // GLA_HEADS
GLA_DV = D_MODEL // GLA_HEADS
GLA_KEY = GLA_HEADS * GLA_DK
GLA_VAL = GLA_HEADS * GLA_DV
GLA_RANK = 16
GLA_TAU = 16.0
GLA_MAIN = 2 * GLA_KEY + 2 * GLA_VAL
GLA_SUB = 16

MEM_LEN = 256
XA_HEADS = 4
XA_HEAD_DIM = 128
XA_DIM = XA_HEADS * XA_HEAD_DIM
MLP_HIDDEN = 4 * D_MODEL


def _cparams(n_axes, vmem=VMEM_LIMIT):
    return pltpu.CompilerParams(dimension_semantics=("arbitrary",) * n_axes,
                                vmem_limit_bytes=vmem)


def _sigmoid(x):
    return 1.0 / (1.0 + jnp.exp(-x))


def _softplus(x):
    return jnp.maximum(x, 0.0) + jnp.log1p(jnp.exp(-jnp.abs(x)))


def _split3(x):
    hi = x.astype(BF16)
    r1 = x - hi.astype(F32)
    mid = r1.astype(BF16)
    lo = (r1 - mid.astype(F32)).astype(BF16)
    return hi, mid, lo


def _dot(a, b):
    return jnp.dot(a, b, preferred_element_type=F32)


def _dot_nt(a, b):
    return lax.dot_general(a, b, (((1,), (1,)), ((), ())), preferred_element_type=F32)


def _dot3_left(m01, x):
    return sum(_dot(m01, p) for p in _split3(x))


def _dot3_right(x, m01):
    return sum(_dot(p, m01) for p in _split3(x))


def _shared_out(kernel_fn, n_in, out_buf):
    if out_buf is None:
        return kernel_fn, [], [], {}

    def body(*refs):
        return kernel_fn(*refs[:n_in], *refs[n_in + 1:])

    return body, [out_buf], [pl.BlockSpec(memory_space=pl.ANY)], {n_in: 0}


def _rmsnorm_kernel(x_ref, g_ref, o_ref):
    x = x_ref[...].astype(F32)
    ms = jnp.mean(x * x, axis=-1, keepdims=True)
    o_ref[...] = (x * lax.rsqrt(ms + EPS) * g_ref[...]).astype(o_ref.dtype)


def _rmsnorm(x, g_stack, layer, out_dtype, bm=256, row0=0, nrows=None):
    m, d = x.shape
    nrows = m if nrows is None else nrows
    assert row0 % bm == 0 and nrows % bm == 0
    blk0 = row0 // bm
    return pl.pallas_call(
        _rmsnorm_kernel,
        grid=(nrows // bm,),
        in_specs=[pl.BlockSpec((bm, d), lambda i: (blk0 + i, 0)),
                  pl.BlockSpec((None, 1, d), lambda i: (layer, 0, 0))],
        out_specs=pl.BlockSpec((bm, d), lambda i: (i, 0)),
        out_shape=jax.ShapeDtypeStruct((nrows, d), out_dtype),
        compiler_params=_cparams(1),
        name="rmsnorm",
    )(x, g_stack)


def _epilogue(mode, acc, acc2):
    if mode == "relu2":
        r = jnp.maximum(acc, 0.0)
        return r * r
    if mode == "glu":
        return acc * _sigmoid(acc2)
    return acc


def _mm_kernel_fullk(*refs, mode, has_res):
    it = iter(refs)
    x_ref = next(it)
    w_ref = next(it)
    w2_ref = next(it) if mode == "glu" else None
    res_ref = next(it) if has_res else None
    o_ref = next(it)
    wbf_ref = next(it)
    wbf2_ref = next(it) if mode == "glu" else None

    @pl.when(pl.program_id(1) == 0)
    def _():
        wbf_ref[...] = w_ref[...].astype(BF16)
        if mode == "glu":
            wbf2_ref[...] = w2_ref[...].astype(BF16)

    x = x_ref[...].astype(BF16)
    acc = _dot(x, wbf_ref[...])
    acc2 = _dot(x, wbf2_ref[...]) if mode == "glu" else None
    out = _epilogue(mode, acc, acc2)
    if has_res:
        out = out + res_ref[...]
    o_ref[...] = out.astype(o_ref.dtype)


def _mm_kernel_splitk(*refs, mode, has_res, nk):
    it = iter(refs)
    x_ref = next(it)
    w_ref = next(it)
    res_ref = next(it) if has_res else None
    o_ref = next(it)
    acc_ref = next(it)
    k = pl.program_id(2)

    @pl.when(k == 0)
    def _():
        acc_ref[...] = jnp.zeros_like(acc_ref)

    acc_ref[...] += _dot(x_ref[...].astype(BF16), w_ref[...].astype(BF16))

    @pl.when(k == nk - 1)
    def _():
        out = _epilogue(mode, acc_ref[...], None)
        if has_res:
            out = out + res_ref[...]
        o_ref[...] = out.astype(o_ref.dtype)


def _matmul(x, w, layer, *, n_out=None, w2=None, res=None, mode="plain", out_dtype=F32,
            bm=1024, bn=512, bk=None):
    m, kdim = x.shape
    n = n_out if n_out is not None else w.shape[2]
    bk = kdim if bk is None else bk
    nk = kdim // bk
    has_res = res is not None
    assert m % bm == 0 and n % bn == 0 and kdim % bk == 0
    if nk == 1:
        grid = (n // bn, m // bm)
        in_specs = [pl.BlockSpec((bm, kdim), lambda j, i: (i, 0)),
                    pl.BlockSpec((None, kdim, bn), lambda j, i: (layer, 0, j))]
        args = [x, w]
        scratch = [pltpu.VMEM((kdim, bn), BF16)]
        if mode == "glu":
            in_specs.append(pl.BlockSpec((None, kdim, bn), lambda j, i: (layer, 0, j)))
            args.append(w2)
            scratch.append(pltpu.VMEM((kdim, bn), BF16))
        if has_res:
            in_specs.append(pl.BlockSpec((bm, bn), lambda j, i: (i, j)))
            args.append(res)
        out_spec = pl.BlockSpec((bm, bn), lambda j, i: (i, j))
        body = functools.partial(_mm_kernel_fullk, mode=mode, has_res=has_res)
        n_axes = 2
    else:
        assert mode != "glu"
        grid = (n // bn, m // bm, nk)
        in_specs = [pl.BlockSpec((bm, bk), lambda j, i, k: (i, k)),
                    pl.BlockSpec((None, bk, bn), lambda j, i, k: (layer, k, j))]
        args = [x, w]
        if has_res:
            in_specs.append(pl.BlockSpec((bm, bn), lambda j, i, k: (i, j)))
            args.append(res)
        out_spec = pl.BlockSpec((bm, bn), lambda j, i, k: (i, j))
        scratch = [pltpu.VMEM((bm, bn), F32)]
        body = functools.partial(_mm_kernel_splitk, mode=mode, has_res=has_res, nk=nk)
        n_axes = 3
    return pl.pallas_call(
        body, grid=grid, in_specs=in_specs, out_specs=out_spec,
        out_shape=jax.ShapeDtypeStruct((m, n), out_dtype),
        scratch_shapes=scratch, compiler_params=_cparams(n_axes),
        name="matmul_" + mode,
    )(*args)


def _xattn_kernel(q_ref, k_ref, v_ref, o_ref, *, bb, lq):
    scale = XA_HEAD_DIM ** -0.5
    for ib in range(bb):
        rows = slice(ib * lq, (ib + 1) * lq)
        for h in range(XA_HEADS):
            sl = slice(h * XA_HEAD_DIM, (h + 1) * XA_HEAD_DIM)
            q = q_ref[rows, sl].astype(BF16)
            mrows = pl.ds(ib * MEM_LEN * XA_HEADS + h, MEM_LEN, stride=XA_HEADS)
            k = k_ref[mrows, :].astype(BF16)
            v = v_ref[mrows, :].astype(BF16)
            s = _dot_nt(q, k) * scale
            p = jnp.exp(s - jnp.max(s, axis=-1, keepdims=True))
            l = jnp.sum(p, axis=-1, keepdims=True)
            o_ref[rows, sl] = (_dot(p.astype(BF16), v) / l).astype(o_ref.dtype)


def _xattn(q, mk, mv, layer, row0, nseq, seq, *, bb, lq, out_buf=None):
    assert bb == 1 or lq == seq
    brows = bb * lq
    assert row0 % brows == 0
    nq = seq // lq
    blk0 = row0 // brows

    def qblk(i, j):
        return (blk0 + i * nq + j, 0)

    in_specs = [pl.BlockSpec((brows, XA_DIM), qblk),
                pl.BlockSpec((None, bb * MEM_LEN * XA_HEADS, XA_HEAD_DIM), lambda i, j: (layer, i, 0)),
                pl.BlockSpec((None, bb * MEM_LEN * XA_HEADS, XA_HEAD_DIM), lambda i, j: (layer, i, 0))]
    body, extra_args, extra_specs, aliases = _shared_out(
        functools.partial(_xattn_kernel, bb=bb, lq=lq), len(in_specs), out_buf)
    return pl.pallas_call(
        body,
        grid=(nseq // bb, nq),
        input_output_aliases=aliases,
        in_specs=in_specs + extra_specs,
        out_specs=pl.BlockSpec((brows, XA_DIM), qblk),
        out_shape=jax.ShapeDtypeStruct(q.shape, F32),
        compiler_params=_cparams(2),
        name="xattn",
    )(q, mk, mv, *extra_args)


def _s5_kernel(u_ref, h0r_ref, h0i_ref, ktr_ref, vc_ref, wc_ref, at_ref, d_ref,
               act_ref, hr_ref, hi_ref, ucat_ref, s_ref, hin_ref, ybuf_ref, bp_ref, cp_ref,
               *, nb, nc, t_len):
    r = nb * nc
    row_g = (lax.broadcasted_iota(jnp.int32, (t_len * LANES, 1), 0) // S5_GROUP) % S5_GPB
    col_h = lax.broadcasted_iota(jnp.int32, (1, S5_SW), 1) // S5_STATE
    for part in range(2):
        pp = vc_ref[0, part]
        tiled = jnp.concatenate([pp] * (S5_SW // LANES), axis=1)
        bp_ref[:, part * S5_SW:(part + 1) * S5_SW] = jnp.where(row_g == col_h, tiled, 0.0).astype(BF16)
    row_h = lax.broadcasted_iota(jnp.int32, (S5_SW, 1), 0) // S5_STATE
    col_g = lax.broadcasted_iota(jnp.int32, (1, LANES), 1) // S5_GROUP
    for t in range(t_len):
        for part in range(2):
            piece = wc_ref[0, t, part * S5_STATE:(part + 1) * S5_STATE, :]
            tiled = jnp.concatenate([piece] * S5_GPB, axis=0)
            cp_ref[t, part * S5_SW:(part + 1) * S5_SW, :] = jnp.where(row_h == col_g, tiled, 0.0).astype(BF16)

    for s in range(t_len):
        ucat_ref[:, s * LANES:(s + 1) * LANES] = u_ref[pl.ds(s, r, stride=t_len), :].astype(BF16)
    half = S5_SW // LANES
    s_all = _dot(ucat_ref[...], bp_ref[...])
    for k in range(2 * half):
        s_ref[k] = s_all[:, k * LANES:(k + 1) * LANES]
    a_sl = [at_ref[0, :, k * LANES:(k + 1) * LANES] for k in range(2 * half)]

    def body(c, carry):
        rows_c = pl.ds(c, nb, stride=nc)
        new = []
        for k in range(half):
            hr, hi = carry[k], carry[half + k]
            hin_ref[k, rows_c, :] = hr
            hin_ref[half + k, rows_c, :] = hi
            new.append((a_sl[k] * hr - a_sl[half + k] * hi + s_ref[k, rows_c, :],
                        a_sl[k] * hi + a_sl[half + k] * hr + s_ref[half + k, rows_c, :]))
        return tuple(n[0] for n in new) + tuple(n[1] for n in new)

    h0 = (tuple(h0r_ref[:, k * LANES:(k + 1) * LANES] for k in range(half))
          + tuple(h0i_ref[:, k * LANES:(k + 1) * LANES] for k in range(half)))
    hfin = lax.fori_loop(0, nc, body, h0)
    for k in range(half):
        hr_ref[:, k * LANES:(k + 1) * LANES] = hfin[k]
        hi_ref[:, k * LANES:(k + 1) * LANES] = hfin[half + k]
    hin = jnp.concatenate([hin_ref[k] for k in range(2 * half)], axis=1).astype(BF16)
    for t in range(t_len):
        y = _dot(hin, cp_ref[t])
        y = y + _dot(ucat_ref[:, :(t + 1) * LANES], ktr_ref[0, (t_len - 1 - t) * LANES:, :])
        ut = u_ref[pl.ds(t, r, stride=t_len), :]
        y = y + d_ref[...] * ut
        ybuf_ref[pl.ds(t, r, stride=t_len), :] = jax.nn.gelu(y)
    act_ref[...] = ybuf_ref[...].astype(act_ref.dtype)


def _s5_prep(a_re, a_im, log_dt, b_re, b_im, c_re, c_im, t_len):
    hp = lax.Precision.HIGHEST
    g, p = a_re.shape
    nblk = g // S5_GPB
    lam_re = a_re.astype(F32)
    lam_im = a_im.astype(F32)
    dt = jnp.exp(log_dt.astype(F32))[:, None]
    zr = lam_re * dt
    zi = lam_im * dt
    ab_re = jnp.exp(zr) * jnp.cos(zi)
    ab_im = jnp.exp(zr) * jnp.sin(zi)
    den = lam_re * lam_re + lam_im * lam_im
    f_re = ((ab_re - 1.0) * lam_re + ab_im * lam_im) / den
    f_im = (ab_im * lam_re - (ab_re - 1.0) * lam_im) / den
    br = b_re.astype(F32)
    bi = b_im.astype(F32)
    bb_re = f_re[..., None] * br - f_im[..., None] * bi
    bb_im = f_re[..., None] * bi + f_im[..., None] * br
    cr = c_re.astype(F32)
    ci = c_im.astype(F32)

    def powers(kk, z_re, z_im):
        mag = jnp.exp(kk * z_re[:, None])
        ang = kk * z_im[:, None]
        return mag * jnp.cos(ang), mag * jnp.sin(ang)

    def rows_gc(x):
        xt = jnp.transpose(x.reshape(nblk, S5_GPB, p, S5_GROUP), (0, 1, 3, 2))
        return jnp.concatenate([xt, xt], axis=-1).reshape(nblk, LANES, 2 * p)

    def rows_g(x):
        xb = jnp.broadcast_to(x.reshape(nblk, S5_GPB, 1, p), (nblk, S5_GPB, S5_GROUP, p))
        return jnp.concatenate([xb, xb], axis=-1).reshape(nblk, LANES, 2 * p)

    kr = (jnp.float32(t_len - 1) - jnp.arange(t_len, dtype=F32)).reshape(1, t_len, 1, 1)
    pr, pi = powers(kr, rows_g(zr), rows_g(zi))
    bbr = rows_gc(bb_re)[:, None]
    bbi = rows_gc(bb_im)[:, None]
    v_re = pr * bbr - pi * bbi
    v_im = pr * bbi + pi * bbr
    bp = jnp.stack([v_re, v_im], axis=1).reshape(nblk, 2, t_len * LANES, LANES)

    def cols_gc(x):
        return jnp.transpose(x.reshape(nblk, S5_GPB, S5_GROUP, p), (0, 3, 1, 2)).reshape(nblk, p, LANES)

    def cols_g(x):
        xt = jnp.transpose(x.reshape(nblk, S5_GPB, p), (0, 2, 1))[..., None]
        return jnp.broadcast_to(xt, (nblk, p, S5_GPB, S5_GROUP)).reshape(nblk, p, LANES)

    kt1 = (1.0 + jnp.arange(t_len, dtype=F32)).reshape(1, t_len, 1, 1)
    qr, qi = powers(kt1, cols_g(zr), cols_g(zi))
    crc = cols_gc(cr)[:, None]
    cic = cols_gc(ci)[:, None]
    cp = jnp.concatenate([crc * qr - cic * qi, -(crc * qi + cic * qr)], axis=2)

    v5r = v_re.reshape(nblk, t_len, S5_GPB, S5_GROUP, 2 * p)[..., :p]
    v5i = v_im.reshape(nblk, t_len, S5_GPB, S5_GROUP, 2 * p)[..., :p]
    cr4 = cr.reshape(nblk, S5_GPB, S5_GROUP, p)
    ci4 = ci.reshape(nblk, S5_GPB, S5_GROUP, p)
    ktt = (jnp.einsum('jmgdp,jgcp->jmgdc', v5r, cr4, precision=hp)
           - jnp.einsum('jmgdp,jgcp->jmgdc', v5i, ci4, precision=hp))
    eye = jnp.eye(S5_GPB, dtype=F32)
    ktr = (ktt[:, :, :, :, None, :] * eye.reshape(1, 1, S5_GPB, 1, S5_GPB, 1)).astype(BF16)
    ktr = ktr.reshape(nblk, t_len * LANES, LANES)

    tt = jnp.float32(t_len)
    at = jnp.concatenate([(jnp.exp(tt * zr) * jnp.cos(tt * zi)).reshape(nblk, 1, S5_SW),
                          (jnp.exp(tt * zr) * jnp.sin(tt * zi)).reshape(nblk, 1, S5_SW)], axis=2)
    return ktr, bp, cp, at


def _s5_core(u, row0, nb, seq, t_len, h0_re, h0_im, ops, d_skip, out_buf=None):
    ktr, bp, cp, at = ops
    nc = seq // t_len
    r = nb * nc
    nrows = nb * seq
    assert row0 % nrows == 0
    rb = row0 // nrows
    h0r = h0_re.reshape(nb, S5_GROUPS * S5_STATE)
    h0i = h0_im.reshape(nb, S5_GROUPS * S5_STATE)
    out_shapes = (jax.ShapeDtypeStruct((u.shape[0], D_MODEL), BF16),
                  jax.ShapeDtypeStruct((nb, S5_GROUPS * S5_STATE), F32),
                  jax.ShapeDtypeStruct((nb, S5_GROUPS * S5_STATE), F32))
    body, extra_args, extra_specs, aliases = _shared_out(
        functools.partial(_s5_kernel, nb=nb, nc=nc, t_len=t_len), 8, out_buf)
    act, hr, hi = pl.pallas_call(
        body,
        grid=(S5_BLOCKS,),
        input_output_aliases=aliases,
        in_specs=[pl.BlockSpec((nrows, LANES), lambda j: (rb, j)),
                  pl.BlockSpec((nb, S5_SW), lambda j: (0, j)),
                  pl.BlockSpec((nb, S5_SW), lambda j: (0, j)),
                  pl.BlockSpec((1, t_len * LANES, LANES), lambda j: (j, 0, 0)),
                  pl.BlockSpec((1, 2, t_len * LANES, LANES), lambda j: (j, 0, 0, 0)),
                  pl.BlockSpec((1, t_len, 2 * S5_STATE, LANES), lambda j: (j, 0, 0, 0)),
                  pl.BlockSpec((1, 1, 2 * S5_SW), lambda j: (j, 0, 0)),
                  pl.BlockSpec((1, LANES), lambda j: (0, j))] + extra_specs,
        out_specs=(pl.BlockSpec((nrows, LANES), lambda j: (rb, j)),
                   pl.BlockSpec((nb, S5_SW), lambda j: (0, j)),
                   pl.BlockSpec((nb, S5_SW), lambda j: (0, j))),
        out_shape=out_shapes,
        scratch_shapes=[pltpu.VMEM((r, t_len * LANES), BF16),
                        pltpu.VMEM((2 * S5_SW // LANES, r, LANES), F32),
                        pltpu.VMEM((2 * S5_SW // LANES, r, LANES), F32),
                        pltpu.VMEM((nrows, LANES), F32),
                        pltpu.VMEM((t_len * LANES, 2 * S5_SW), BF16),
                        pltpu.VMEM((t_len, 2 * S5_SW, LANES), BF16)],
        compiler_params=_cparams(1),
        name="s5_core",
    )(u, h0r, h0i, ktr, bp, cp, at, d_skip.reshape(1, D_MODEL), *extra_args)
    return (act, hr.reshape(nb, S5_GROUPS, S5_STATE), hi.reshape(nb, S5_GROUPS, S5_STATE))


def _ssd_kernel(z_ref, x_ref, b_ref, c_ref, dt_ref, dtt_ref,
                cwx_ref, cwb_ref, cwc_ref, cbx_ref, cbb_ref, cbc_ref,
                csx_ref, csb_ref, csc_ref,
                dtb_ref, dtbt_ref, alog_ref, alogt_ref, dexp_ref, norm_ref, st0_ref,
                y_ref, st_ref, xbuf, bbuf, cbuf, *, nb, qt):
    rows = nb * qt
    c = pl.program_id(2)
    pad = 8
    hist = SSD_CONV - 1

    @pl.when(c == 0)
    def _():
        st_ref[...] = st0_ref[...]
        xbuf[:, pad - hist:pad, :] = csx_ref[...]
        bbuf[:, pad - hist:pad, :] = csb_ref[...]
        cbuf[:, pad - hist:pad, :] = csc_ref[...]

    def conv(buf, raw_ref, w_ref, bias_ref, width):
        buf[:, pad:, :] = raw_ref[...].reshape(nb, qt, width)
        acc = bias_ref[...].reshape(1, 1, width)
        for k in range(SSD_CONV):
            acc = acc + buf[:, pad - hist + k:pad - hist + k + qt, :] * w_ref[k:k + 1, :].reshape(1, 1, width)
        tail = buf[:, pad + qt - hist:pad + qt, :]
        buf[:, pad - hist:pad, :] = tail
        acc = acc * _sigmoid(acc)
        return acc.reshape(rows, width)

    xs = conv(xbuf, x_ref, cwx_ref, cbx_ref, SSD_GW)
    bs = conv(bbuf, b_ref, cwb_ref, cbb_ref, SSD_STATE).astype(BF16)
    cs = conv(cbuf, c_ref, cwc_ref, cbc_ref, SSD_STATE).astype(BF16)

    dt = _softplus(dt_ref[0] + dtb_ref[0])
    dtt = _softplus(dtt_ref[0] + dtbt_ref[0])
    da = dt * (-jnp.exp(alog_ref[0]))
    dat = dtt * (-jnp.exp(alogt_ref[0]))

    ri = lax.broadcasted_iota(jnp.int32, (rows, rows), 0)
    ci = lax.broadcasted_iota(jnp.int32, (rows, rows), 1)
    same = (ri // qt) == (ci // qt)
    tri = same & (ri >= ci)
    tri_bf = tri.astype(BF16)
    trit_bf = (same & (ri <= ci)).astype(BF16)
    same_bf = same.astype(BF16)
    cum = _dot3_left(tri_bf, da)
    cumt = _dot3_right(dat, trit_bf)
    tot = _dot3_left(same_bf, da)
    cb = _dot_nt(cs, bs)

    lane = lax.broadcasted_iota(jnp.int32, (1, LANES), 1)
    lo = lane < SSD_HEAD_DIM
    srow = lax.broadcasted_iota(jnp.int32, (LANES, 1), 0) < SSD_HEAD_DIM
    row_seq = lax.broadcasted_iota(jnp.int32, (rows, 1), 0) // qt
    neg_inf = jnp.float32(-jnp.inf)
    bs_seq = [bs if nb == 1 else jnp.where(row_seq == ib, bs, jnp.zeros_like(bs)) for ib in range(nb)]

    ys = []
    for p in range(SSD_HPG // 2):
        ra, rb = 2 * p, 2 * p + 1
        ws = []
        for r in (ra, rb):
            seg = cum[:, r:r + 1] - cumt[r:r + 1, :]
            dec = jnp.exp(jnp.where(tri, seg, neg_inf))
            ws.append((cb * dec * dtt[r:r + 1, :]).astype(BF16))
        xp = xs[:, p * LANES:(p + 1) * LANES]
        y = (_dot(ws[0], jnp.where(lo, xp, 0.0).astype(BF16))
             + _dot(ws[1], jnp.where(lo, 0.0, xp).astype(BF16)))
        ecol = jnp.where(lo, jnp.exp(cum[:, ra:ra + 1]), jnp.exp(cum[:, rb:rb + 1]))
        inter = []
        for ib in range(nb):
            st = st_ref[ib, 0, p * LANES:(p + 1) * LANES, :].astype(BF16)
            inter.append(_dot_nt(cs[ib * qt:(ib + 1) * qt, :], st))
        inter = inter[0] if nb == 1 else jnp.concatenate(inter, axis=0)
        y = y + inter * ecol + dexp_ref[:, p * LANES:(p + 1) * LANES] * xp
        ys.append(y)

        sc = jnp.where(lo, jnp.exp(tot[:, ra:ra + 1] - cum[:, ra:ra + 1]) * dt[:, ra:ra + 1],
                       jnp.exp(tot[:, rb:rb + 1] - cum[:, rb:rb + 1]) * dt[:, rb:rb + 1])
        xwt = jnp.transpose(xp * sc).astype(BF16)
        for ib in range(nb):
            upd = _dot(xwt, bs_seq[ib])
            r0 = ib * qt
            dcol = jnp.where(srow, jnp.exp(tot[r0:r0 + 1, ra:ra + 1]), jnp.exp(tot[r0:r0 + 1, rb:rb + 1]))
            old = st_ref[ib, 0, p * LANES:(p + 1) * LANES, :]
            st_ref[ib, 0, p * LANES:(p + 1) * LANES, :] = old * dcol + upd

    y = jnp.concatenate(ys, axis=1)
    zg = z_ref[...]
    y = y * (zg * _sigmoid(zg))
    ms = jnp.mean(y * y, axis=-1, keepdims=True)
    y_ref[...] = (y * lax.rsqrt(ms + EPS) * norm_ref[...]).astype(y_ref.dtype)


def _ssd_core(proj, dtp, dtt, row_blk0, nseq, nb, qt, nc, st0, conv0, params, out_buf=None):
    cw, cbias, dtb, dtbt, alog, alogt, dexp, norm = params
    nbb = nseq // nb
    xoff = SSD_INNER // SSD_GW
    boff = (SSD_INNER + SSD_INNER) // SSD_STATE
    coff = boff + SSD_GN // SSD_STATE
    cwb_off = SSD_INNER // SSD_STATE
    cwc_off = cwb_off + SSD_GN // SSD_STATE

    def rblk(i, g, c):
        return row_blk0 + i * nc + c

    st0 = st0.reshape(nseq, SSD_GROUPS, SSD_GW, SSD_STATE)
    hist = SSD_CONV - 1
    in_specs = [
        pl.BlockSpec((ROWS, SSD_GW), lambda i, g, c: (rblk(i, g, c), g)),
        pl.BlockSpec((ROWS, SSD_GW), lambda i, g, c: (rblk(i, g, c), xoff + g)),
        pl.BlockSpec((ROWS, SSD_STATE), lambda i, g, c: (rblk(i, g, c), boff + g)),
        pl.BlockSpec((ROWS, SSD_STATE), lambda i, g, c: (rblk(i, g, c), coff + g)),
        pl.BlockSpec((1, ROWS, LANES), lambda i, g, c: (g, rblk(i, g, c), 0)),
        pl.BlockSpec((1, SSD_HPG, ROWS), lambda i, g, c: (g, 0, rblk(i, g, c))),
        pl.BlockSpec((SSD_CONV, SSD_GW), lambda i, g, c: (0, g)),
        pl.BlockSpec((SSD_CONV, SSD_STATE), lambda i, g, c: (0, cwb_off + g)),
        pl.BlockSpec((SSD_CONV, SSD_STATE), lambda i, g, c: (0, cwc_off + g)),
        pl.BlockSpec((1, SSD_GW), lambda i, g, c: (0, g)),
        pl.BlockSpec((1, SSD_STATE), lambda i, g, c: (0, cwb_off + g)),
        pl.BlockSpec((1, SSD_STATE), lambda i, g, c: (0, cwc_off + g)),
        pl.BlockSpec((nb, hist, SSD_GW), lambda i, g, c: (i, 0, g)),
        pl.BlockSpec((nb, hist, SSD_STATE), lambda i, g, c: (i, 0, cwb_off + g)),
        pl.BlockSpec((nb, hist, SSD_STATE), lambda i, g, c: (i, 0, cwc_off + g)),
        pl.BlockSpec((1, 1, LANES), lambda i, g, c: (g, 0, 0)),
        pl.BlockSpec((1, SSD_HPG, 1), lambda i, g, c: (g, 0, 0)),
        pl.BlockSpec((1, 1, LANES), lambda i, g, c: (g, 0, 0)),
        pl.BlockSpec((1, SSD_HPG, 1), lambda i, g, c: (g, 0, 0)),
        pl.BlockSpec((1, SSD_GW), lambda i, g, c: (0, g)),
        pl.BlockSpec((1, SSD_GW), lambda i, g, c: (0, g)),
        pl.BlockSpec((nb, 1, SSD_GW, SSD_STATE), lambda i, g, c: (i, g, 0, 0)),
    ]
    out_specs = (pl.BlockSpec((ROWS, SSD_GW), lambda i, g, c: (rblk(i, g, c), g)),
                 pl.BlockSpec((nb, 1, SSD_GW, SSD_STATE), lambda i, g, c: (i, g, 0, 0)))
    out_shape = (jax.ShapeDtypeStruct((proj.shape[0], SSD_INNER), BF16),
                 jax.ShapeDtypeStruct((nseq, SSD_GROUPS, SSD_GW, SSD_STATE), F32))
    body, extra_args, extra_specs, aliases = _shared_out(
        functools.partial(_ssd_kernel, nb=nb, qt=qt), len(in_specs), out_buf)
    y, st = pl.pallas_call(
        body,
        grid=(nbb, SSD_GROUPS, nc),
        input_output_aliases=aliases,
        in_specs=in_specs + extra_specs, out_specs=out_specs, out_shape=out_shape,
        scratch_shapes=[pltpu.VMEM((nb, 8 + qt, SSD_GW), F32),
                        pltpu.VMEM((nb, 8 + qt, SSD_STATE), F32),
                        pltpu.VMEM((nb, 8 + qt, SSD_STATE), F32)],
        compiler_params=_cparams(3),
        name="ssd_core",
    )(proj, proj, proj, proj, dtp, dtt, cw, cw, cw, cbias, cbias, cbias,
      conv0, conv0, conv0, dtb, dtbt, alog, alogt, dexp, norm, st0, *extra_args)
    return y, st.reshape(nseq, SSD_HEADS, SSD_HEAD_DIM, SSD_STATE)


def _gla_kernel(q_ref, k_ref, v_ref, r_ref, lr_ref, wa2_ref, ba_ref, norm_ref, s0_ref,
                o_ref, s_ref, cum_s, q_s, k_s, att_s, *, nb, qt, wq):
    rows = nb * qt
    c = pl.program_id(2)

    @pl.when(c == 0)
    def _():
        s_ref[...] = s0_ref[...]

    qv = q_ref[...] * (GLA_DK ** -0.5)
    kv = k_ref[...]
    v_bf = v_ref[...].astype(BF16)
    gate = _dot(lr_ref[...].astype(BF16), wa2_ref[...].astype(BF16)) + ba_ref[...]
    g = -_softplus(-gate) / GLA_TAU

    ri = lax.broadcasted_iota(jnp.int32, (rows, rows), 0)
    ci = lax.broadcasted_iota(jnp.int32, (rows, rows), 1)
    same = (ri // qt) == (ci // qt)
    tri_bf = (same & (ri >= ci)).astype(BF16)
    same_bf = same.astype(BF16)
    cum = _dot3_left(tri_bf, g)
    tot = _dot3_left(same_bf, g)
    cum_s[...] = cum
    q_s[...] = qv
    k_s[...] = kv

    qe = (qv * jnp.exp(cum)).astype(BF16)
    o_parts = [_dot(qe[ib * qt:(ib + 1) * qt, :], s_ref[ib, 0].astype(BF16)) for ib in range(nb)]
    o = o_parts[0] if nb == 1 else jnp.concatenate(o_parts, axis=0)

    neg_inf = jnp.float32(-jnp.inf)
    att_s[...] = jnp.zeros_like(att_s)
    if qt > wq:
        assert nb == 1
        rowi = lax.broadcasted_iota(jnp.int32, (rows, 1), 0)
        for blk in range(1, qt // wq):
            lo_r, hi_r = blk * wq, (blk + 1) * wq
            c0 = cum[lo_r - 1:lo_r, :]
            q_blk = (qv[lo_r:hi_r, :] * jnp.exp(cum[lo_r:hi_r, :] - c0)).astype(BF16)
            k_all = (kv * jnp.exp(jnp.where(rowi < lo_r, c0 - cum, neg_inf))).astype(BF16)
            att_s[lo_r:hi_r, :] = _dot_nt(q_blk, k_all)

    lane = lax.broadcasted_iota(jnp.int32, (1, rows), 1)
    wrow = lax.broadcasted_iota(jnp.int32, (wq, 1), 0)

    def window(w, carry):
        w0 = pl.multiple_of(w * wq, wq)
        cw = cum_s[pl.ds(w0, wq), :]
        qw = q_s[pl.ds(w0, wq), :]
        kw = k_s[pl.ds(w0, wq), :]
        blk = att_s[pl.ds(w0, wq), :]
        for jj in range(wq):
            e = jnp.exp(jnp.where(wrow >= jj, cw - cw[jj:jj + 1, :], neg_inf))
            col = jnp.sum(qw * e * kw[jj:jj + 1, :], axis=-1, keepdims=True)
            blk = jnp.where(lane == w0 + jj, col, blk)
        att_s[pl.ds(w0, wq), :] = blk
        return carry

    lax.fori_loop(0, rows // wq, window, 0)
    o = o + _dot(att_s[...].astype(BF16), v_bf)

    kdt = jnp.transpose(kv * jnp.exp(tot - cum)).astype(BF16)
    gt = jnp.transpose(g)
    rsel = lax.broadcasted_iota(jnp.int32, (rows, LANES), 0) // qt
    row_seq = lax.broadcasted_iota(jnp.int32, (rows, 1), 0) // qt
    for ib in range(nb):
        sel = (rsel == ib).astype(BF16)
        dec = jnp.exp(_dot3_right(gt, sel))
        dec = jnp.concatenate([dec] * (GLA_DV // LANES), axis=1)
        v_ib = v_bf if nb == 1 else jnp.where(row_seq == ib, v_bf, jnp.zeros_like(v_bf))
        s_ref[ib, 0] = s_ref[ib, 0] * dec + _dot(kdt, v_ib)

    ms = jnp.mean(o * o, axis=-1, keepdims=True)
    o = o * lax.rsqrt(ms + EPS) * norm_ref[...]
    rr = r_ref[...]
    o_ref[...] = (o * (rr * _sigmoid(rr))).astype(o_ref.dtype)


def _gla_core(proj, lrp, row_blk0, nseq, nb, qt, nc, wq, s0, params, out_buf=None):
    wa2p, ba, norm = params
    nbb = nseq // nb
    koff = GLA_KEY // GLA_DK
    voff = 2 * GLA_KEY // GLA_DV
    roff = voff + GLA_VAL // GLA_DV

    def rblk(i, h, c):
        return row_blk0 + i * nc + c

    in_specs = [
        pl.BlockSpec((ROWS, GLA_DK), lambda i, h, c: (rblk(i, h, c), h)),
        pl.BlockSpec((ROWS, GLA_DK), lambda i, h, c: (rblk(i, h, c), koff + h)),
        pl.BlockSpec((ROWS, GLA_DV), lambda i, h, c: (rblk(i, h, c), voff + h)),
        pl.BlockSpec((ROWS, GLA_DV), lambda i, h, c: (rblk(i, h, c), roff + h)),
        pl.BlockSpec((ROWS, LANES), lambda i, h, c: (rblk(i, h, c), 0)),
        pl.BlockSpec((LANES, GLA_DK), lambda i, h, c: (0, h)),
        pl.BlockSpec((1, GLA_DK), lambda i, h, c: (0, h)),
        pl.BlockSpec((1, GLA_DV), lambda i, h, c: (0, 0)),
        pl.BlockSpec((nb, 1, GLA_DK, GLA_DV), lambda i, h, c: (i, h, 0, 0)),
    ]
    out_specs = (pl.BlockSpec((ROWS, GLA_DV), lambda i, h, c: (rblk(i, h, c), h)),
                 pl.BlockSpec((nb, 1, GLA_DK, GLA_DV), lambda i, h, c: (i, h, 0, 0)))
    out_shape = (jax.ShapeDtypeStruct((proj.shape[0], GLA_VAL), BF16),
                 jax.ShapeDtypeStruct((nseq, GLA_HEADS, GLA_DK, GLA_DV), F32))
    body, extra_args, extra_specs, aliases = _shared_out(
        functools.partial(_gla_kernel, nb=nb, qt=qt, wq=wq), len(in_specs), out_buf)
    return pl.pallas_call(
        body,
        grid=(nbb, GLA_HEADS, nc),
        input_output_aliases=aliases,
        in_specs=in_specs + extra_specs, out_specs=out_specs, out_shape=out_shape,
        scratch_shapes=[pltpu.VMEM((ROWS, GLA_DK), F32), pltpu.VMEM((ROWS, GLA_DK), F32),
                        pltpu.VMEM((ROWS, GLA_DK), F32), pltpu.VMEM((ROWS, ROWS), F32)],
        compiler_params=_cparams(3),
        name="gla_core",
    )(proj, proj, proj, proj, lrp, wa2p, ba, norm, s0, *extra_args)


def kernel(x_prompt, x_sample, state_s5_re, state_s5_im, state_ssd, state_ssd_conv, state_gla, cache_mem_k, cache_mem_v, mem_prompt, norm_mix, norm_xattn, norm_mem, norm_mlp, norm_final, xa_w_q, xa_w_k, xa_w_v, xa_w_o, mlp_w_up, mlp_w_down, s5_a_re, s5_a_im, s5_log_dt, s5_b_re, s5_b_im, s5_c_re, s5_c_im, s5_d, s5_glu_w1, s5_glu_w2, ssd_w_in, ssd_conv_w, ssd_conv_b, ssd_dt_bias, ssd_a_log, ssd_d, ssd_norm, ssd_w_out, gla_w_in, gla_w_a2, gla_b_a, gla_norm, gla_w_out):
    pb, pseq, d = x_prompt.shape
    sb, sseq, _ = x_sample.shape
    mp = pb * pseq
    ms = sb * sseq
    assert mp % ROWS == 0 and ms % ROWS == 0 and pseq % ROWS == 0 and ROWS % sseq == 0
    x = jnp.concatenate([x_prompt.reshape(mp, d), x_sample.reshape(ms, d)], axis=0)
    m = mp + ms
    p_blk0 = 0
    s_blk0 = mp // ROWS
    s_nb = ROWS // sseq

    g_mix = norm_mix.reshape(DEPTH, 1, d)
    g_xa = norm_xattn.reshape(DEPTH, 1, d)
    g_mem = norm_mem.reshape(DEPTH, 1, d)
    g_mlp = norm_mlp.reshape(DEPTH, 1, d)

    mem2 = mem_prompt.reshape(pb * MEM_LEN, d)
    pk, pv = [], []
    for i in range(DEPTH):
        mn = _rmsnorm(mem2, g_mem, i, BF16)
        pk.append(_matmul(mn, xa_w_k, i, bm=512, bn=XA_DIM))
        pv.append(_matmul(mn, xa_w_v, i, bm=512, bn=XA_DIM))
    p_mem_k = jnp.stack(pk).reshape(DEPTH, pb * MEM_LEN * XA_HEADS, XA_HEAD_DIM)
    p_mem_v = jnp.stack(pv).reshape(DEPTH, pb * MEM_LEN * XA_HEADS, XA_HEAD_DIM)
    c_mem_k = cache_mem_k.reshape(DEPTH, sb * MEM_LEN * XA_HEADS, XA_HEAD_DIM)
    c_mem_v = cache_mem_v.reshape(DEPTH, sb * MEM_LEN * XA_HEADS, XA_HEAD_DIM)

    outs = {k: [] for k in ("pa_re", "pa_im", "pb", "pbc", "pc", "sa_re", "sa_im", "sb", "sbc", "sc")}
    ia = ib = ic = 0
    for i in range(DEPTH):
        kind = i % 3
        if kind == 0:
            hn = _rmsnorm(x, g_mix, i, F32)
            ops_p = _s5_prep(s5_a_re[ia], s5_a_im[ia], s5_log_dt[ia], s5_b_re[ia], s5_b_im[ia],
                             s5_c_re[ia], s5_c_im[ia], 16)
            ops_s = _s5_prep(s5_a_re[ia], s5_a_im[ia], s5_log_dt[ia], s5_b_re[ia], s5_b_im[ia],
                             s5_c_re[ia], s5_c_im[ia], sseq)
            zero = jnp.zeros((pb, S5_GROUPS, S5_STATE), F32)
            act, hr_p, hi_p = _s5_core(hn, 0, pb, pseq, 16, zero, zero, ops_p, s5_d[ia])
            act, hr_s, hi_s = _s5_core(hn, mp, sb, sseq, sseq, state_s5_re[ia], state_s5_im[ia],
                                       ops_s, s5_d[ia], out_buf=act)
            outs["pa_re"].append(hr_p)
            outs["pa_im"].append(hi_p)
            outs["sa_re"].append(hr_s)
            outs["sa_im"].append(hi_s)
            x = _matmul(act, s5_glu_w1, ia, w2=s5_glu_w2, res=x, mode="glu", bn=256)
            ia += 1
        elif kind == 1:
            hn = _rmsnorm(x, g_mix, i, BF16)
            proj = _matmul(hn, ssd_w_in, ib, bn=640)
            dt_raw = proj[:, SSD_INNER + SSD_CONV_DIM:].reshape(m, SSD_GROUPS, SSD_HPG)
            dtt = jnp.transpose(dt_raw, (1, 2, 0))
            dtp = jnp.pad(jnp.transpose(dt_raw, (1, 0, 2)), ((0, 0), (0, 0), (0, LANES - SSD_HPG)))
            pad_h = ((0, 0), (0, 0), (0, LANES - SSD_HPG))
            params = (ssd_conv_w[ib], ssd_conv_b[ib].reshape(1, SSD_CONV_DIM),
                      jnp.pad(ssd_dt_bias[ib].reshape(SSD_GROUPS, 1, SSD_HPG), pad_h),
                      ssd_dt_bias[ib].reshape(SSD_GROUPS, SSD_HPG, 1),
                      jnp.pad(ssd_a_log[ib].reshape(SSD_GROUPS, 1, SSD_HPG), pad_h),
                      ssd_a_log[ib].reshape(SSD_GROUPS, SSD_HPG, 1),
                      jnp.repeat(ssd_d[ib], SSD_HEAD_DIM).reshape(1, SSD_INNER),
                      ssd_norm[ib].reshape(1, SSD_INNER))
            yb, st_p = _ssd_core(proj, dtp, dtt, p_blk0, pb, 1, ROWS, pseq // ROWS,
                                 jnp.zeros((pb, SSD_HEADS, SSD_HEAD_DIM, SSD_STATE), F32),
                                 jnp.zeros((pb, SSD_CONV - 1, SSD_CONV_DIM), F32), params)
            yb, st_s = _ssd_core(proj, dtp, dtt, s_blk0, sb, s_nb, sseq, 1,
                                 state_ssd[ib], state_ssd_conv[ib], params, out_buf=yb)
            tail = jnp.arange(SSD_CONV - 1)
            rows_p = (jnp.arange(pb)[:, None] * pseq + (pseq - (SSD_CONV - 1)) + tail[None, :]).reshape(-1)
            rows_s = (mp + jnp.arange(sb)[:, None] * sseq + (sseq - (SSD_CONV - 1)) + tail[None, :]).reshape(-1)
            xbc_p = jnp.take(proj, rows_p, axis=0)[:, SSD_INNER:SSD_INNER + SSD_CONV_DIM]
            xbc_s = jnp.take(proj, rows_s, axis=0)[:, SSD_INNER:SSD_INNER + SSD_CONV_DIM]
            outs["pb"].append(st_p)
            outs["sb"].append(st_s)
            outs["pbc"].append(xbc_p.reshape(pb, SSD_CONV - 1, SSD_CONV_DIM))
            outs["sbc"].append(xbc_s.reshape(sb, SSD_CONV - 1, SSD_CONV_DIM))
            x = _matmul(yb, ssd_w_out, ib, res=x, bm=1152, bn=1024, bk=1024)
            ib += 1
        else:
            hn = _rmsnorm(x, g_mix, i, BF16)
            proj = _matmul(hn, gla_w_in, ic, n_out=GLA_MAIN)
            w_lr = jnp.pad(gla_w_in[ic][:, GLA_MAIN:], ((0, 0), (0, LANES - GLA_RANK)))[None]
            lrp = _matmul(hn, w_lr, 0, bn=LANES)
            params = (jnp.pad(gla_w_a2[ic], ((0, LANES - GLA_RANK), (0, 0))),
                      gla_b_a[ic].reshape(1, GLA_KEY), gla_norm[ic].reshape(1, GLA_DV))
            ob, s_p = _gla_core(proj, lrp, p_blk0, pb, 1, ROWS, pseq // ROWS, GLA_SUB,
                                jnp.zeros((pb, GLA_HEADS, GLA_DK, GLA_DV), F32), params)
            ob, s_s = _gla_core(proj, lrp, s_blk0, sb, s_nb, sseq, 1, sseq, state_gla[ic], params,
                                out_buf=ob)
            outs["pc"].append(s_p)
            outs["sc"].append(s_s)
            x = _matmul(ob, gla_w_out, ic, res=x)
            ic += 1

        hx = _rmsnorm(x, g_xa, i, BF16)
        qp = _matmul(hx, xa_w_q, i, bn=XA_DIM)
        oc = _xattn(qp, p_mem_k, p_mem_v, i, 0, pb, pseq, bb=1, lq=512)
        oc = _xattn(qp, c_mem_k, c_mem_v, i, mp, sb, sseq, bb=8, lq=sseq, out_buf=oc)
        x = _matmul(oc, xa_w_o, i, res=x)

        hm = _rmsnorm(x, g_mlp, i, BF16)
        hid = _matmul(hm, mlp_w_up, i, mode="relu2", out_dtype=BF16)
        x = _matmul(hid, mlp_w_down, i, res=x, bm=1152, bn=1024, bk=1024)

    g_fin = norm_final.reshape(1, 1, d)
    y_prompt = _rmsnorm(x, g_fin, 0, F32, row0=0, nrows=mp).reshape(pb, pseq, d)
    y_sample = _rmsnorm(x, g_fin, 0, F32, row0=mp, nrows=ms).reshape(sb, sseq, d)
    shp5 = (DEPTH, -1, MEM_LEN, XA_HEADS, XA_HEAD_DIM)
    return (y_prompt, y_sample,
            jnp.stack(outs["pa_re"]), jnp.stack(outs["pa_im"]), jnp.stack(outs["pb"]),
            jnp.stack(outs["pbc"]), jnp.stack(outs["pc"]),
            p_mem_k.reshape(shp5), p_mem_v.reshape(shp5),
            jnp.stack(outs["sa_re"]), jnp.stack(outs["sa_im"]), jnp.stack(outs["sb"]),
            jnp.stack(outs["sbc"]), jnp.stack(outs["sc"]))
```

```python
import functools
import math

import jax
import jax.numpy as jnp
from jax import lax
from jax.experimental import pallas as pl
from jax.experimental.pallas import tpu as pltpu

F32 = jnp.float32
BF16 = jnp.bfloat16
EPS = 1e-5

D_MODEL = 4096
DEPTH = 4
LANES = 128
ROWS = 128
VMEM_LIMIT = 56 * 1024 * 1024

S5_GROUP = 16
S5_GROUPS = D_MODEL // S5_GROUP
S5_STATE = 64
S5_GPB = LANES // S5_GROUP
S5_BLOCKS = D_MODEL // LANES
S5_SW = S5_GPB * S5_STATE

SSD_INNER = 2 * D_MODEL
SSD_HEAD_DIM = 64
SSD_HEADS = SSD_INNER // SSD_HEAD_DIM
SSD_GROUPS = 8
SSD_HPG = SSD_HEADS // SSD_GROUPS
SSD_STATE = 128
SSD_CONV = 4
SSD_GN = SSD_GROUPS * SSD_STATE
SSD_CONV_DIM = SSD_INNER + 2 * SSD_GN
SSD_IN_DIM = SSD_INNER + SSD_CONV_DIM + SSD_HEADS
SSD_GW = SSD_INNER // SSD_GROUPS

GLA_HEADS = 8
GLA_DK = D_MODEL // 2 // GLA_HEADS
GLA_DV = D_MODEL // GLA_HEADS
GLA_KEY = GLA_HEADS * GLA_DK
GLA_VAL = GLA_HEADS * GLA_DV
GLA_RANK = 16
GLA_TAU = 16.0
GLA_MAIN = 2 * GLA_KEY + 2 * GLA_VAL
GLA_SUB = 16

MEM_LEN = 256
XA_HEADS = 4
XA_HEAD_DIM = 128
XA_DIM = XA_HEADS * XA_HEAD_DIM
MLP_HIDDEN = 4 * D_MODEL


def _cparams(n_axes, vmem=VMEM_LIMIT):
    return pltpu.CompilerParams(dimension_semantics=("arbitrary",) * n_axes,
                                vmem_limit_bytes=vmem)


def _sigmoid(x):
    return 1.0 / (1.0 + jnp.exp(-x))


def _softplus(x):
    return jnp.maximum(x, 0.0) + jnp.log1p(jnp.exp(-jnp.abs(x)))


def _split3(x):
    hi = x.astype(BF16)
    r1 = x - hi.astype(F32)
    mid = r1.astype(BF16)
    lo = (r1 - mid.astype(F32)).astype(BF16)
    return hi, mid, lo


def _dot(a, b):
    return jnp.dot(a, b, preferred_element_type=F32)


def _dot_nt(a, b):
    return lax.dot_general(a, b, (((1,), (1,)), ((), ())), preferred_element_type=F32)


def _dot3_left(m01, x):
    return sum(_dot(m01, p) for p in _split3(x))


def _dot3_right(x, m01):
    return sum(_dot(p, m01) for p in _split3(x))


def _shared_out(kernel_fn, n_in, out_buf):
    if out_buf is None:
        return kernel_fn, [], [], {}

    def body(*refs):
        return kernel_fn(*refs[:n_in], *refs[n_in + 1:])

    return body, [out_buf], [pl.BlockSpec(memory_space=pl.ANY)], {n_in: 0}


def _rmsnorm_kernel(x_ref, g_ref, o_ref):
    x = x_ref[...].astype(F32)
    ms = jnp.mean(x * x, axis=-1, keepdims=True)
    o_ref[...] = (x * lax.rsqrt(ms + EPS) * g_ref[...]).astype(o_ref.dtype)


def _rmsnorm(x, g_stack, layer, out_dtype, bm=256, row0=0, nrows=None):
    m, d = x.shape
    nrows = m if nrows is None else nrows
    assert row0 % bm == 0 and nrows % bm == 0
    blk0 = row0 // bm
    return pl.pallas_call(
        _rmsnorm_kernel,
        grid=(nrows // bm,),
        in_specs=[pl.BlockSpec((bm, d), lambda i: (blk0 + i, 0)),
                  pl.BlockSpec((None, 1, d), lambda i: (layer, 0, 0))],
        out_specs=pl.BlockSpec((bm, d), lambda i: (i, 0)),
        out_shape=jax.ShapeDtypeStruct((nrows, d), out_dtype),
        compiler_params=_cparams(1),
        name="rmsnorm",
    )(x, g_stack)


def _epilogue(mode, acc, acc2):
    if mode == "relu2":
        r = jnp.maximum(acc, 0.0)
        return r * r
    if mode == "glu":
        return acc * _sigmoid(acc2)
    return acc


def _mm_kernel_fullk(*refs, mode, has_res):
    it = iter(refs)
    x_ref = next(it)
    w_ref = next(it)
    w2_ref = next(it) if mode == "glu" else None
    res_ref = next(it) if has_res else None
    o_ref = next(it)
    wbf_ref = next(it)
    wbf2_ref = next(it) if mode == "glu" else None

    @pl.when(pl.program_id(1) == 0)
    def _():
        wbf_ref[...] = w_ref[...].astype(BF16)
        if mode == "glu":
            wbf2_ref[...] = w2_ref[...].astype(BF16)

    x = x_ref[...].astype(BF16)
    acc = _dot(x, wbf_ref[...])
    acc2 = _dot(x, wbf2_ref[...]) if mode == "glu" else None
    out = _epilogue(mode, acc, acc2)
    if has_res:
        out = out + res_ref[...]
    o_ref[...] = out.astype(o_ref.dtype)


def _mm_kernel_splitk(*refs, mode, has_res, nk):
    it = iter(refs)
    x_ref = next(it)
    w_ref = next(it)
    res_ref = next(it) if has_res else None
    o_ref = next(it)
    acc_ref = next(it)
    k = pl.program_id(2)

    @pl.when(k == 0)
    def _():
        acc_ref[...] = jnp.zeros_like(acc_ref)

    acc_ref[...] += _dot(x_ref[...].astype(BF16), w_ref[...].astype(BF16))

    @pl.when(k == nk - 1)
    def _():
        out = _epilogue(mode, acc_ref[...], None)
        if has_res:
            out = out + res_ref[...]
        o_ref[...] = out.astype(o_ref.dtype)


def _matmul(x, w, layer, *, n_out=None, w2=None, res=None, mode="plain", out_dtype=F32,
            bm=1024, bn=512, bk=None, w_single_buffer=False):
    m, kdim = x.shape
    n = n_out if n_out is not None else w.shape[2]
    bk = kdim if bk is None else bk
    nk = kdim // bk
    has_res = res is not None
    assert m % bm == 0 and n % bn == 0 and kdim % bk == 0
    if nk == 1:
        grid = (n // bn, m // bm)
        w_mode = {"pipeline_mode": pl.Buffered(1)} if w_single_buffer else {}
        in_specs = [pl.BlockSpec((bm, kdim), lambda j, i: (i, 0)),
                    pl.BlockSpec((None, kdim, bn), lambda j, i: (layer, 0, j), **w_mode)]
        args = [x, w]
        scratch = [pltpu.VMEM((kdim, bn), BF16)]
        if mode == "glu":
            in_specs.append(pl.BlockSpec((None, kdim, bn), lambda j, i: (layer, 0, j)))
            args.append(w2)
            scratch.append(pltpu.VMEM((kdim, bn), BF16))
        if has_res:
            in_specs.append(pl.BlockSpec((bm, bn), lambda j, i: (i, j)))
            args.append(res)
        out_spec = pl.BlockSpec((bm, bn), lambda j, i: (i, j))
        body = functools.partial(_mm_kernel_fullk, mode=mode, has_res=has_res)
        n_axes = 2
    else:
        assert mode != "glu"
        grid = (n // bn, m // bm, nk)
        in_specs = [pl.BlockSpec((bm, bk), lambda j, i, k: (i, k)),
                    pl.BlockSpec((None, bk, bn), lambda j, i, k: (layer, k, j))]
        args = [x, w]
        if has_res:
            in_specs.append(pl.BlockSpec((bm, bn), lambda j, i, k: (i, j)))
            args.append(res)
        out_spec = pl.BlockSpec((bm, bn), lambda j, i, k: (i, j))
        scratch = [pltpu.VMEM((bm, bn), F32)]
        body = functools.partial(_mm_kernel_splitk, mode=mode, has_res=has_res, nk=nk)
        n_axes = 3
    return pl.pallas_call(
        body, grid=grid, in_specs=in_specs, out_specs=out_spec,
        out_shape=jax.ShapeDtypeStruct((m, n), out_dtype),
        scratch_shapes=scratch, compiler_params=_cparams(n_axes),
        name="matmul_" + mode,
    )(*args)


def _xattn_kernel(q_ref, k_ref, v_ref, o_ref, *, bb, lq):
    scale = XA_HEAD_DIM ** -0.5
    for ib in range(bb):
        rows = slice(ib * lq, (ib + 1) * lq)
        for h in range(XA_HEADS):
            sl = slice(h * XA_HEAD_DIM, (h + 1) * XA_HEAD_DIM)
            q = q_ref[rows, sl].astype(BF16)
            mrows = pl.ds(ib * MEM_LEN * XA_HEADS + h, MEM_LEN, stride=XA_HEADS)
            k = k_ref[mrows, :].astype(BF16)
            v = v_ref[mrows, :].astype(BF16)
            s = _dot_nt(q, k) * scale
            p = jnp.exp(s - jnp.max(s, axis=-1, keepdims=True))
            l = jnp.sum(p, axis=-1, keepdims=True)
            o_ref[rows, sl] = (_dot(p.astype(BF16), v) / l).astype(o_ref.dtype)


def _xattn(q, mk, mv, layer, row0, nseq, seq, *, bb, lq, out_buf=None):
    assert bb == 1 or lq == seq
    brows = bb * lq
    assert row0 % brows == 0
    nq = seq // lq
    blk0 = row0 // brows

    def qblk(i, j):
        return (blk0 + i * nq + j, 0)

    in_specs = [pl.BlockSpec((brows, XA_DIM), qblk),
                pl.BlockSpec((None, bb * MEM_LEN * XA_HEADS, XA_HEAD_DIM), lambda i, j: (layer, i, 0)),
                pl.BlockSpec((None, bb * MEM_LEN * XA_HEADS, XA_HEAD_DIM), lambda i, j: (layer, i, 0))]
    body, extra_args, extra_specs, aliases = _shared_out(
        functools.partial(_xattn_kernel, bb=bb, lq=lq), len(in_specs), out_buf)
    return pl.pallas_call(
        body,
        grid=(nseq // bb, nq),
        input_output_aliases=aliases,
        in_specs=in_specs + extra_specs,
        out_specs=pl.BlockSpec((brows, XA_DIM), qblk),
        out_shape=jax.ShapeDtypeStruct(q.shape, F32),
        compiler_params=_cparams(2),
        name="xattn",
    )(q, mk, mv, *extra_args)


def _s5_kernel(u_ref, h0r_ref, h0i_ref, ktr_ref, vc_ref, wc_ref, at_ref, d_ref,
               act_ref, hr_ref, hi_ref, ucat_ref, s_ref, hin_ref, ybuf_ref, bp_ref, cp_ref,
               *, nb, nc, t_len):
    r = nb * nc
    row_g = (lax.broadcasted_iota(jnp.int32, (t_len * LANES, 1), 0) // S5_GROUP) % S5_GPB
    col_h = lax.broadcasted_iota(jnp.int32, (1, S5_SW), 1) // S5_STATE
    for part in range(2):
        pp = vc_ref[0, part]
        tiled = jnp.concatenate([pp] * (S5_SW // LANES), axis=1)
        bp_ref[:, part * S5_SW:(part + 1) * S5_SW] = jnp.where(row_g == col_h, tiled, 0.0).astype(BF16)
    row_h = lax.broadcasted_iota(jnp.int32, (S5_SW, 1), 0) // S5_STATE
    col_g = lax.broadcasted_iota(jnp.int32, (1, LANES), 1) // S5_GROUP
    for t in range(t_len):
        for part in range(2):
            piece = wc_ref[0, t, part * S5_STATE:(part + 1) * S5_STATE, :]
            tiled = jnp.concatenate([piece] * S5_GPB, axis=0)
            cp_ref[t, part * S5_SW:(part + 1) * S5_SW, :] = jnp.where(row_h == col_g, tiled, 0.0).astype(BF16)

    for s in range(t_len):
        ucat_ref[:, s * LANES:(s + 1) * LANES] = u_ref[pl.ds(s, r, stride=t_len), :].astype(BF16)
    half = S5_SW // LANES
    s_all = _dot(ucat_ref[...], bp_ref[...])
    for k in range(2 * half):
        s_ref[k] = s_all[:, k * LANES:(k + 1) * LANES]
    a_sl = [at_ref[0, :, k * LANES:(k + 1) * LANES] for k in range(2 * half)]

    def body(c, carry):
        rows_c = pl.ds(c, nb, stride=nc)
        new = []
        for k in range(half):
            hr, hi = carry[k], carry[half + k]
            hin_ref[k, rows_c, :] = hr
            hin_ref[half + k, rows_c, :] = hi
            new.append((a_sl[k] * hr - a_sl[half + k] * hi + s_ref[k, rows_c, :],
                        a_sl[k] * hi + a_sl[half + k] * hr + s_ref[half + k, rows_c, :]))
        return tuple(n[0] for n in new) + tuple(n[1] for n in new)

    h0 = (tuple(h0r_ref[:, k * LANES:(k + 1) * LANES] for k in range(half))
          + tuple(h0i_ref[:, k * LANES:(k + 1) * LANES] for k in range(half)))
    hfin = lax.fori_loop(0, nc, body, h0)
    for k in range(half):
        hr_ref[:, k * LANES:(k + 1) * LANES] = hfin[k]
        hi_ref[:, k * LANES:(k + 1) * LANES] = hfin[half + k]
    hin = jnp.concatenate([hin_ref[k] for k in range(2 * half)], axis=1).astype(BF16)
    for t in range(t_len):
        y = _dot(hin, cp_ref[t])
        y = y + _dot(ucat_ref[:, :(t + 1) * LANES], ktr_ref[0, (t_len - 1 - t) * LANES:, :])
        ut = u_ref[pl.ds(t, r, stride=t_len), :]
        y = y + d_ref[...] * ut
        ybuf_ref[pl.ds(t, r, stride=t_len), :] = jax.nn.gelu(y)
    act_ref[...] = ybuf_ref[...].astype(act_ref.dtype)


def _s5_prep(a_re, a_im, log_dt, b_re, b_im, c_re, c_im, t_len):
    hp = lax.Precision.HIGHEST
    g, p = a_re.shape
    nblk = g // S5_GPB
    lam_re = a_re.astype(F32)
    lam_im = a_im.astype(F32)
    dt = jnp.exp(log_dt.astype(F32))[:, None]
    zr = lam_re * dt
    zi = lam_im * dt
    ab_re = jnp.exp(zr) * jnp.cos(zi)
    ab_im = jnp.exp(zr) * jnp.sin(zi)
    den = lam_re * lam_re + lam_im * lam_im
    f_re = ((ab_re - 1.0) * lam_re + ab_im * lam_im) / den
    f_im = (ab_im * lam_re - (ab_re - 1.0) * lam_im) / den
    br = b_re.astype(F32)
    bi = b_im.astype(F32)
    bb_re = f_re[..., None] * br - f_im[..., None] * bi
    bb_im = f_re[..., None] * bi + f_im[..., None] * br
    cr = c_re.astype(F32)
    ci = c_im.astype(F32)

    def powers(kk):
        mag = jnp.exp(kk * zr[None])
        ang = kk * zi[None]
        return mag * jnp.cos(ang), mag * jnp.sin(ang)

    def rows_gc(x):
        xt = jnp.transpose(x.reshape(nblk, S5_GPB, p, S5_GROUP), (0, 1, 3, 2))
        return jnp.concatenate([xt, xt], axis=-1).reshape(nblk, LANES, 2 * p)

    def rows_g(x):
        xb = jnp.transpose(x.reshape(t_len, nblk, S5_GPB, 1, 1, p), (1, 0, 2, 3, 4, 5))
        xb = jnp.broadcast_to(xb, (nblk, t_len, S5_GPB, S5_GROUP, 2, p))
        return xb.reshape(nblk, t_len, LANES, 2 * p)

    kr = (jnp.float32(t_len - 1) - jnp.arange(t_len, dtype=F32)).reshape(t_len, 1, 1)
    pr, pi = (rows_g(a) for a in powers(kr))
    bbr = rows_gc(bb_re)[:, None]
    bbi = rows_gc(bb_im)[:, None]
    v_re = pr * bbr - pi * bbi
    v_im = pr * bbi + pi * bbr
    bp = jnp.stack([v_re, v_im], axis=1).reshape(nblk, 2, t_len * LANES, LANES)

    def cols_gc(x):
        return jnp.transpose(x.reshape(nblk, S5_GPB, S5_GROUP, p), (0, 3, 1, 2)).reshape(nblk, p, LANES)

    def cols_g(x):
        xt = jnp.transpose(x.reshape(t_len, nblk, S5_GPB, p), (1, 0, 3, 2))[..., None]
        return jnp.broadcast_to(xt, (nblk, t_len, p, S5_GPB, S5_GROUP)).reshape(nblk, t_len, p, LANES)

    kt1 = (1.0 + jnp.arange(t_len, dtype=F32)).reshape(t_len, 1, 1)
    qr, qi = (cols_g(a) for a in powers(kt1))
    crc = cols_gc(cr)[:, None]
    cic = cols_gc(ci)[:, None]
    cp = jnp.concatenate([crc * qr - cic * qi, -(crc * qi + cic * qr)], axis=2)

    v5r = v_re.reshape(nblk, t_len, S5_GPB, S5_GROUP, 2 * p)[..., :p]
    v5i = v_im.reshape(nblk, t_len, S5_GPB, S5_GROUP, 2 * p)[..., :p]
    cr4 = cr.reshape(nblk, S5_GPB, S5_GROUP, p)
    ci4 = ci.reshape(nblk, S5_GPB, S5_GROUP, p)
    ktt = (jnp.einsum('jmgdp,jgcp->jmgdc', v5r, cr4, precision=hp)
           - jnp.einsum('jmgdp,jgcp->jmgdc', v5i, ci4, precision=hp))
    eye = jnp.eye(S5_GPB, dtype=F32)
    ktr = (ktt[:, :, :, :, None, :] * eye.reshape(1, 1, S5_GPB, 1, S5_GPB, 1)).astype(BF16)
    ktr = ktr.reshape(nblk, t_len * LANES, LANES)

    tt = jnp.float32(t_len)
    at = jnp.concatenate([(jnp.exp(tt * zr) * jnp.cos(tt * zi)).reshape(nblk, 1, S5_SW),
                          (jnp.exp(tt * zr) * jnp.sin(tt * zi)).reshape(nblk, 1, S5_SW)], axis=2)
    return ktr, bp, cp, at


def _s5_core(u, row0, nb, seq, t_len, h0_re, h0_im, ops, d_skip, out_buf=None):
    ktr, bp, cp, at = ops
    nc = seq // t_len
    r = nb * nc
    nrows = nb * seq
    assert row0 % nrows == 0
    rb = row0 // nrows
    h0r = h0_re.reshape(nb, S5_GROUPS * S5_STATE)
    h0i = h0_im.reshape(nb, S5_GROUPS * S5_STATE)
    out_shapes = (jax.ShapeDtypeStruct((u.shape[0], D_MODEL), BF16),
                  jax.ShapeDtypeStruct((nb, S5_GROUPS * S5_STATE), F32),
                  jax.ShapeDtypeStruct((nb, S5_GROUPS * S5_STATE), F32))
    body, extra_args, extra_specs, aliases = _shared_out(
        functools.partial(_s5_kernel, nb=nb, nc=nc, t_len=t_len), 8, out_buf)
    act, hr, hi = pl.pallas_call(
        body,
        grid=(S5_BLOCKS,),
        input_output_aliases=aliases,
        in_specs=[pl.BlockSpec((nrows, LANES), lambda j: (rb, j)),
                  pl.BlockSpec((nb, S5_SW), lambda j: (0, j)),
                  pl.BlockSpec((nb, S5_SW), lambda j: (0, j)),
                  pl.BlockSpec((1, t_len * LANES, LANES), lambda j: (j, 0, 0)),
                  pl.BlockSpec((1, 2, t_len * LANES, LANES), lambda j: (j, 0, 0, 0)),
                  pl.BlockSpec((1, t_len, 2 * S5_STATE, LANES), lambda j: (j, 0, 0, 0)),
                  pl.BlockSpec((1, 1, 2 * S5_SW), lambda j: (j, 0, 0)),
                  pl.BlockSpec((1, LANES), lambda j: (0, j))] + extra_specs,
        out_specs=(pl.BlockSpec((nrows, LANES), lambda j: (rb, j)),
                   pl.BlockSpec((nb, S5_SW), lambda j: (0, j)),
                   pl.BlockSpec((nb, S5_SW), lambda j: (0, j))),
        out_shape=out_shapes,
        scratch_shapes=[pltpu.VMEM((r, t_len * LANES), BF16),
                        pltpu.VMEM((2 * S5_SW // LANES, r, LANES), F32),
                        pltpu.VMEM((2 * S5_SW // LANES, r, LANES), F32),
                        pltpu.VMEM((nrows, LANES), F32),
                        pltpu.VMEM((t_len * LANES, 2 * S5_SW), BF16),
                        pltpu.VMEM((t_len, 2 * S5_SW, LANES), BF16)],
        compiler_params=_cparams(1),
        name="s5_core",
    )(u, h0r, h0i, ktr, bp, cp, at, d_skip.reshape(1, D_MODEL), *extra_args)
    return (act, hr.reshape(nb, S5_GROUPS, S5_STATE), hi.reshape(nb, S5_GROUPS, S5_STATE))


def _ssd_kernel(z_ref, x_ref, b_ref, c_ref, dt_ref, dtt_ref,
                cwx_ref, cwb_ref, cwc_ref, cbx_ref, cbb_ref, cbc_ref,
                csx_ref, csb_ref, csc_ref,
                dtb_ref, dtbt_ref, alog_ref, alogt_ref, dexp_ref, norm_ref, st0_ref,
                y_ref, st_ref, xbuf, bbuf, cbuf, *, nb, qt):
    rows = nb * qt
    c = pl.program_id(2)
    pad = 8
    hist = SSD_CONV - 1

    @pl.when(c == 0)
    def _():
        st_ref[...] = st0_ref[...]
        xbuf[:, pad - hist:pad, :] = csx_ref[...]
        bbuf[:, pad - hist:pad, :] = csb_ref[...]
        cbuf[:, pad - hist:pad, :] = csc_ref[...]

    def conv(buf, raw_ref, w_ref, bias_ref, width):
        buf[:, pad:, :] = raw_ref[...].reshape(nb, qt, width)
        acc = bias_ref[...].reshape(1, 1, width)
        for k in range(SSD_CONV):
            acc = acc + buf[:, pad - hist + k:pad - hist + k + qt, :] * w_ref[k:k + 1, :].reshape(1, 1, width)
        tail = buf[:, pad + qt - hist:pad + qt, :]
        buf[:, pad - hist:pad, :] = tail
        acc = acc * _sigmoid(acc)
        return acc.reshape(rows, width)

    xs = conv(xbuf, x_ref, cwx_ref, cbx_ref, SSD_GW)
    bs = conv(bbuf, b_ref, cwb_ref, cbb_ref, SSD_STATE).astype(BF16)
    cs = conv(cbuf, c_ref, cwc_ref, cbc_ref, SSD_STATE).astype(BF16)

    dt = _softplus(dt_ref[0] + dtb_ref[0])
    dtt = _softplus(dtt_ref[0] + dtbt_ref[0])
    da = dt * (-jnp.exp(alog_ref[0]))
    dat = dtt * (-jnp.exp(alogt_ref[0]))

    ri = lax.broadcasted_iota(jnp.int32, (rows, rows), 0)
    ci = lax.broadcasted_iota(jnp.int32, (rows, rows), 1)
    same = (ri // qt) == (ci // qt)
    tri = same & (ri >= ci)
    tri_bf = tri.astype(BF16)
    trit_bf = (same & (ri <= ci)).astype(BF16)
    same_bf = same.astype(BF16)
    cum = _dot3_left(tri_bf, da)
    cumt = _dot3_right(dat, trit_bf)
    tot = _dot3_left(same_bf, da)
    cb = _dot_nt(cs, bs)

    lane = lax.broadcasted_iota(jnp.int32, (1, LANES), 1)
    lo = lane < SSD_HEAD_DIM
    srow = lax.broadcasted_iota(jnp.int32, (LANES, 1), 0) < SSD_HEAD_DIM
    row_seq = lax.broadcasted_iota(jnp.int32, (rows, 1), 0) // qt
    neg_inf = jnp.float32(-jnp.inf)
    bs_seq = [bs if nb == 1 else jnp.where(row_seq == ib, bs, jnp.zeros_like(bs)) for ib in range(nb)]

    ys = []
    for p in range(SSD_HPG // 2):
        ra, rb = 2 * p, 2 * p + 1
        ws = []
        for r in (ra, rb):
            seg = cum[:, r:r + 1] - cumt[r:r + 1, :]
            dec = jnp.exp(jnp.where(tri, seg, neg_inf))
            ws.append((cb * dec * dtt[r:r + 1, :]).astype(BF16))
        xp = xs[:, p * LANES:(p + 1) * LANES]
        y = (_dot(ws[0], jnp.where(lo, xp, 0.0).astype(BF16))
             + _dot(ws[1], jnp.where(lo, 0.0, xp).astype(BF16)))
        ecol = jnp.where(lo, jnp.exp(cum[:, ra:ra + 1]), jnp.exp(cum[:, rb:rb + 1]))
        inter = []
        for ib in range(nb):
            st = st_ref[ib, 0, p * LANES:(p + 1) * LANES, :].astype(BF16)
            inter.append(_dot_nt(cs[ib * qt:(ib + 1) * qt, :], st))
        inter = inter[0] if nb == 1 else jnp.concatenate(inter, axis=0)
        y = y + inter * ecol + dexp_ref[:, p * LANES:(p + 1) * LANES] * xp
        ys.append(y)

        sc = jnp.where(lo, jnp.exp(tot[:, ra:ra + 1] - cum[:, ra:ra + 1]) * dt[:, ra:ra + 1],
                       jnp.exp(tot[:, rb:rb + 1] - cum[:, rb:rb + 1]) * dt[:, rb:rb + 1])
        xwt = jnp.transpose(xp * sc).astype(BF16)
        for ib in range(nb):
            upd = _dot(xwt, bs_seq[ib])
            r0 = ib * qt
            dcol = jnp.where(srow, jnp.exp(tot[r0:r0 + 1, ra:ra + 1]), jnp.exp(tot[r0:r0 + 1, rb:rb + 1]))
            old = st_ref[ib, 0, p * LANES:(p + 1) * LANES, :]
            st_ref[ib, 0, p * LANES:(p + 1) * LANES, :] = old * dcol + upd

    y = jnp.concatenate(ys, axis=1)
    zg = z_ref[...]
    y = y * (zg * _sigmoid(zg))
    ms = jnp.mean(y * y, axis=-1, keepdims=True)
    y_ref[...] = (y * lax.rsqrt(ms + EPS) * norm_ref[...]).astype(y_ref.dtype)


def _ssd_core(proj, dtp, dtt, row_blk0, nseq, nb, qt, nc, st0, conv0, params, out_buf=None):
    cw, cbias, dtb, dtbt, alog, alogt, dexp, norm = params
    nbb = nseq // nb
    xoff = SSD_INNER // SSD_GW
    boff = (SSD_INNER + SSD_INNER) // SSD_STATE
    coff = boff + SSD_GN // SSD_STATE
    cwb_off = SSD_INNER // SSD_STATE
    cwc_off = cwb_off + SSD_GN // SSD_STATE

    def rblk(i, g, c):
        return row_blk0 + i * nc + c

    st0 = st0.reshape(nseq, SSD_GROUPS, SSD_GW, SSD_STATE)
    hist = SSD_CONV - 1
    in_specs = [
        pl.BlockSpec((ROWS, SSD_GW), lambda i, g, c: (rblk(i, g, c), g)),
        pl.BlockSpec((ROWS, SSD_GW), lambda i, g, c: (rblk(i, g, c), xoff + g)),
        pl.BlockSpec((ROWS, SSD_STATE), lambda i, g, c: (rblk(i, g, c), boff + g)),
        pl.BlockSpec((ROWS, SSD_STATE), lambda i, g, c: (rblk(i, g, c), coff + g)),
        pl.BlockSpec((1, ROWS, LANES), lambda i, g, c: (g, rblk(i, g, c), 0)),
        pl.BlockSpec((1, SSD_HPG, ROWS), lambda i, g, c: (g, 0, rblk(i, g, c))),
        pl.BlockSpec((SSD_CONV, SSD_GW), lambda i, g, c: (0, g)),
        pl.BlockSpec((SSD_CONV, SSD_STATE), lambda i, g, c: (0, cwb_off + g)),
        pl.BlockSpec((SSD_CONV, SSD_STATE), lambda i, g, c: (0, cwc_off + g)),
        pl.BlockSpec((1, SSD_GW), lambda i, g, c: (0, g)),
        pl.BlockSpec((1, SSD_STATE), lambda i, g, c: (0, cwb_off + g)),
        pl.BlockSpec((1, SSD_STATE), lambda i, g, c: (0, cwc_off + g)),
        pl.BlockSpec((nb, hist, SSD_GW), lambda i, g, c: (i, 0, g)),
        pl.BlockSpec((nb, hist, SSD_STATE), lambda i, g, c: (i, 0, cwb_off + g)),
        pl.BlockSpec((nb, hist, SSD_STATE), lambda i, g, c: (i, 0, cwc_off + g)),
        pl.BlockSpec((1, 1, LANES), lambda i, g, c: (g, 0, 0)),
        pl.BlockSpec((1, SSD_HPG, 1), lambda i, g, c: (g, 0, 0)),
        pl.BlockSpec((1, 1, LANES), lambda i, g, c: (g, 0, 0)),
        pl.BlockSpec((1, SSD_HPG, 1), lambda i, g, c: (g, 0, 0)),
        pl.BlockSpec((1, SSD_GW), lambda i, g, c: (0, g)),
        pl.BlockSpec((1, SSD_GW), lambda i, g, c: (0, g)),
        pl.BlockSpec((nb, 1, SSD_GW, SSD_STATE), lambda i, g, c: (i, g, 0, 0)),
    ]
    out_specs = (pl.BlockSpec((ROWS, SSD_GW), lambda i, g, c: (rblk(i, g, c), g)),
                 pl.BlockSpec((nb, 1, SSD_GW, SSD_STATE), lambda i, g, c: (i, g, 0, 0)))
    out_shape = (jax.ShapeDtypeStruct((proj.shape[0], SSD_INNER), BF16),
                 jax.ShapeDtypeStruct((nseq, SSD_GROUPS, SSD_GW, SSD_STATE), F32))
    body, extra_args, extra_specs, aliases = _shared_out(
        functools.partial(_ssd_kernel, nb=nb, qt=qt), len(in_specs), out_buf)
    y, st = pl.pallas_call(
        body,
        grid=(nbb, SSD_GROUPS, nc),
        input_output_aliases=aliases,
        in_specs=in_specs + extra_specs, out_specs=out_specs, out_shape=out_shape,
        scratch_shapes=[pltpu.VMEM((nb, 8 + qt, SSD_GW), F32),
                        pltpu.VMEM((nb, 8 + qt, SSD_STATE), F32),
                        pltpu.VMEM((nb, 8 + qt, SSD_STATE), F32)],
        compiler_params=_cparams(3),
        name="ssd_core",
    )(proj, proj, proj, proj, dtp, dtt, cw, cw, cw, cbias, cbias, cbias,
      conv0, conv0, conv0, dtb, dtbt, alog, alogt, dexp, norm, st0, *extra_args)
    return y, st.reshape(nseq, SSD_HEADS, SSD_HEAD_DIM, SSD_STATE)


def _gla_kernel(q_ref, k_ref, v_ref, r_ref, lr_ref, wa2_ref, ba_ref, norm_ref, s0_ref,
                o_ref, s_ref, cum_s, q_s, k_s, att_s, *, nb, qt, wq):
    rows = nb * qt
    c = pl.program_id(2)

    @pl.when(c == 0)
    def _():
        s_ref[...] = s0_ref[...]

    qv = q_ref[...] * (GLA_DK ** -0.5)
    kv = k_ref[...]
    v_bf = v_ref[...].astype(BF16)
    gate = _dot(lr_ref[...].astype(BF16), wa2_ref[...].astype(BF16)) + ba_ref[...]
    g = -_softplus(-gate) / GLA_TAU

    ri = lax.broadcasted_iota(jnp.int32, (rows, rows), 0)
    ci = lax.broadcasted_iota(jnp.int32, (rows, rows), 1)
    same = (ri // qt) == (ci // qt)
    tri_bf = (same & (ri >= ci)).astype(BF16)
    same_bf = same.astype(BF16)
    cum = _dot3_left(tri_bf, g)
    tot = _dot3_left(same_bf, g)
    cum_s[...] = cum
    q_s[...] = qv
    k_s[...] = kv

    qe = (qv * jnp.exp(cum)).astype(BF16)
    o_parts = [_dot(qe[ib * qt:(ib + 1) * qt, :], s_ref[ib, 0].astype(BF16)) for ib in range(nb)]
    o = o_parts[0] if nb == 1 else jnp.concatenate(o_parts, axis=0)

    neg_inf = jnp.float32(-jnp.inf)
    att_s[...] = jnp.zeros_like(att_s)
    if qt > wq:
        assert nb == 1
        rowi = lax.broadcasted_iota(jnp.int32, (rows, 1), 0)
        for blk in range(1, qt // wq):
            lo_r, hi_r = blk * wq, (blk + 1) * wq
            c0 = cum[lo_r - 1:lo_r, :]
            q_blk = (qv[lo_r:hi_r, :] * jnp.exp(cum[lo_r:hi_r, :] - c0)).astype(BF16)
            k_all = (kv * jnp.exp(jnp.where(rowi < lo_r, c0 - cum, neg_inf))).astype(BF16)
            att_s[lo_r:hi_r, :] = _dot_nt(q_blk, k_all)

    lane = lax.broadcasted_iota(jnp.int32, (1, rows), 1)
    wrow = lax.broadcasted_iota(jnp.int32, (wq, 1), 0)

    def window(w, carry):
        w0 = pl.multiple_of(w * wq, wq)
        cw = cum_s[pl.ds(w0, wq), :]
        qw = q_s[pl.ds(w0, wq), :]
        kw = k_s[pl.ds(w0, wq), :]
        blk = att_s[pl.ds(w0, wq), :]
        for jj in range(wq):
            e = jnp.exp(jnp.where(wrow >= jj, cw - cw[jj:jj + 1, :], neg_inf))
            col = jnp.sum(qw * e * kw[jj:jj + 1, :], axis=-1, keepdims=True)
            blk = jnp.where(lane == w0 + jj, col, blk)
        att_s[pl.ds(w0, wq), :] = blk
        return carry

    lax.fori_loop(0, rows // wq, window, 0)
    o = o + _dot(att_s[...].astype(BF16), v_bf)

    kdt = jnp.transpose(kv * jnp.exp(tot - cum)).astype(BF16)
    gt = jnp.transpose(g)
    rsel = lax.broadcasted_iota(jnp.int32, (rows, LANES), 0) // qt
    row_seq = lax.broadcasted_iota(jnp.int32, (rows, 1), 0) // qt
    for ib in range(nb):
        sel = (rsel == ib).astype(BF16)
        dec = jnp.exp(_dot3_right(gt, sel))
        dec = jnp.concatenate([dec] * (GLA_DV // LANES), axis=1)
        v_ib = v_bf if nb == 1 else jnp.where(row_seq == ib, v_bf, jnp.zeros_like(v_bf))
        s_ref[ib, 0] = s_ref[ib, 0] * dec + _dot(kdt, v_ib)

    ms = jnp.mean(o * o, axis=-1, keepdims=True)
    o = o * lax.rsqrt(ms + EPS) * norm_ref[...]
    rr = r_ref[...]
    o_ref[...] = (o * (rr * _sigmoid(rr))).astype(o_ref.dtype)


def _gla_core(proj, lrp, row_blk0, nseq, nb, qt, nc, wq, s0, params, out_buf=None):
    wa2p, ba, norm = params
    nbb = nseq // nb
    koff = GLA_KEY // GLA_DK
    voff = 2 * GLA_KEY // GLA_DV
    roff = voff + GLA_VAL // GLA_DV

    def rblk(i, h, c):
        return row_blk0 + i * nc + c

    in_specs = [
        pl.BlockSpec((ROWS, GLA_DK), lambda i, h, c: (rblk(i, h, c), h)),
        pl.BlockSpec((ROWS, GLA_DK), lambda i, h, c: (rblk(i, h, c), koff + h)),
        pl.BlockSpec((ROWS, GLA_DV), lambda i, h, c: (rblk(i, h, c), voff + h)),
        pl.BlockSpec((ROWS, GLA_DV), lambda i, h, c: (rblk(i, h, c), roff + h)),
        pl.BlockSpec((ROWS, LANES), lambda i, h, c: (rblk(i, h, c), 0)),
        pl.BlockSpec((LANES, GLA_DK), lambda i, h, c: (0, h)),
        pl.BlockSpec((1, GLA_DK), lambda i, h, c: (0, h)),
        pl.BlockSpec((1, GLA_DV), lambda i, h, c: (0, 0)),
        pl.BlockSpec((nb, 1, GLA_DK, GLA_DV), lambda i, h, c: (i, h, 0, 0)),
    ]
    out_specs = (pl.BlockSpec((ROWS, GLA_DV), lambda i, h, c: (rblk(i, h, c), h)),
                 pl.BlockSpec((nb, 1, GLA_DK, GLA_DV), lambda i, h, c: (i, h, 0, 0)))
    out_shape = (jax.ShapeDtypeStruct((proj.shape[0], GLA_VAL), BF16),
                 jax.ShapeDtypeStruct((nseq, GLA_HEADS, GLA_DK, GLA_DV), F32))
    body, extra_args, extra_specs, aliases = _shared_out(
        functools.partial(_gla_kernel, nb=nb, qt=qt, wq=wq), len(in_specs), out_buf)
    return pl.pallas_call(
        body,
        grid=(nbb, GLA_HEADS, nc),
        input_output_aliases=aliases,
        in_specs=in_specs + extra_specs, out_specs=out_specs, out_shape=out_shape,
        scratch_shapes=[pltpu.VMEM((ROWS, GLA_DK), F32), pltpu.VMEM((ROWS, GLA_DK), F32),
                        pltpu.VMEM((ROWS, GLA_DK), F32), pltpu.VMEM((ROWS, ROWS), F32)],
        compiler_params=_cparams(3),
        name="gla_core",
    )(proj, proj, proj, proj, lrp, wa2p, ba, norm, s0, *extra_args)


def kernel(x_prompt, x_sample, state_s5_re, state_s5_im, state_ssd, state_ssd_conv, state_gla, cache_mem_k, cache_mem_v, mem_prompt, norm_mix, norm_xattn, norm_mem, norm_mlp, norm_final, xa_w_q, xa_w_k, xa_w_v, xa_w_o, mlp_w_up, mlp_w_down, s5_a_re, s5_a_im, s5_log_dt, s5_b_re, s5_b_im, s5_c_re, s5_c_im, s5_d, s5_glu_w1, s5_glu_w2, ssd_w_in, ssd_conv_w, ssd_conv_b, ssd_dt_bias, ssd_a_log, ssd_d, ssd_norm, ssd_w_out, gla_w_in, gla_w_a2, gla_b_a, gla_norm, gla_w_out):
    pb, pseq, d = x_prompt.shape
    sb, sseq, _ = x_sample.shape
    mp = pb * pseq
    ms = sb * sseq
    assert mp % ROWS == 0 and ms % ROWS == 0 and pseq % ROWS == 0 and ROWS % sseq == 0
    x = jnp.concatenate([x_prompt.reshape(mp, d), x_sample.reshape(ms, d)], axis=0)
    m = mp + ms
    p_blk0 = 0
    s_blk0 = mp // ROWS
    s_nb = ROWS // sseq

    g_mix = norm_mix.reshape(DEPTH, 1, d)
    g_xa = norm_xattn.reshape(DEPTH, 1, d)
    g_mem = norm_mem.reshape(DEPTH, 1, d)
    g_mlp = norm_mlp.reshape(DEPTH, 1, d)

    mem2 = mem_prompt.reshape(pb * MEM_LEN, d)
    pk, pv = [], []
    for i in range(DEPTH):
        mn = _rmsnorm(mem2, g_mem, i, BF16)
        pk.append(_matmul(mn, xa_w_k, i, bm=512, bn=XA_DIM))
        pv.append(_matmul(mn, xa_w_v, i, bm=512, bn=XA_DIM))
    p_mem_k = jnp.stack(pk).reshape(DEPTH, pb * MEM_LEN * XA_HEADS, XA_HEAD_DIM)
    p_mem_v = jnp.stack(pv).reshape(DEPTH, pb * MEM_LEN * XA_HEADS, XA_HEAD_DIM)
    c_mem_k = cache_mem_k.reshape(DEPTH, sb * MEM_LEN * XA_HEADS, XA_HEAD_DIM)
    c_mem_v = cache_mem_v.reshape(DEPTH, sb * MEM_LEN * XA_HEADS, XA_HEAD_DIM)

    outs = {k: [] for k in ("pa_re", "pa_im", "pb", "pbc", "pc", "sa_re", "sa_im", "sb", "sbc", "sc")}
    ia = ib = ic = 0
    for i in range(DEPTH):
        kind = i % 3
        if kind == 0:
            hn = _rmsnorm(x, g_mix, i, F32)
            ops_p = _s5_prep(s5_a_re[ia], s5_a_im[ia], s5_log_dt[ia], s5_b_re[ia], s5_b_im[ia],
                             s5_c_re[ia], s5_c_im[ia], 16)
            ops_s = _s5_prep(s5_a_re[ia], s5_a_im[ia], s5_log_dt[ia], s5_b_re[ia], s5_b_im[ia],
                             s5_c_re[ia], s5_c_im[ia], sseq)
            zero = jnp.zeros((pb, S5_GROUPS, S5_STATE), F32)
            act, hr_p, hi_p = _s5_core(hn, 0, pb, pseq, 16, zero, zero, ops_p, s5_d[ia])
            act, hr_s, hi_s = _s5_core(hn, mp, sb, sseq, sseq, state_s5_re[ia], state_s5_im[ia],
                                       ops_s, s5_d[ia], out_buf=act)
            outs["pa_re"].append(hr_p)
            outs["pa_im"].append(hi_p)
            outs["sa_re"].append(hr_s)
            outs["sa_im"].append(hi_s)
            x = _matmul(act, s5_glu_w1, ia, w2=s5_glu_w2, res=x, mode="glu", bn=256)
            ia += 1
        elif kind == 1:
            hn = _rmsnorm(x, g_mix, i, BF16)
            proj = _matmul(hn, ssd_w_in, ib, bn=640)
            dt_raw = proj[:, SSD_INNER + SSD_CONV_DIM:].reshape(m, SSD_GROUPS, SSD_HPG)
            dtt = jnp.transpose(dt_raw, (1, 2, 0))
            dtp = jnp.pad(jnp.transpose(dt_raw, (1, 0, 2)), ((0, 0), (0, 0), (0, LANES - SSD_HPG)))
            pad_h = ((0, 0), (0, 0), (0, LANES - SSD_HPG))
            params = (ssd_conv_w[ib], ssd_conv_b[ib].reshape(1, SSD_CONV_DIM),
                      jnp.pad(ssd_dt_bias[ib].reshape(SSD_GROUPS, 1, SSD_HPG), pad_h),
                      ssd_dt_bias[ib].reshape(SSD_GROUPS, SSD_HPG, 1),
                      jnp.pad(ssd_a_log[ib].reshape(SSD_GROUPS, 1, SSD_HPG), pad_h),
                      ssd_a_log[ib].reshape(SSD_GROUPS, SSD_HPG, 1),
                      jnp.repeat(ssd_d[ib], SSD_HEAD_DIM).reshape(1, SSD_INNER),
                      ssd_norm[ib].reshape(1, SSD_INNER))
            yb, st_p = _ssd_core(proj, dtp, dtt, p_blk0, pb, 1, ROWS, pseq // ROWS,
                                 jnp.zeros((pb, SSD_HEADS, SSD_HEAD_DIM, SSD_STATE), F32),
                                 jnp.zeros((pb, SSD_CONV - 1, SSD_CONV_DIM), F32), params)
            yb, st_s = _ssd_core(proj, dtp, dtt, s_blk0, sb, s_nb, sseq, 1,
                                 state_ssd[ib], state_ssd_conv[ib], params, out_buf=yb)
            tail = jnp.arange(SSD_CONV - 1)
            rows_p = (jnp.arange(pb)[:, None] * pseq + (pseq - (SSD_CONV - 1)) + tail[None, :]).reshape(-1)
            rows_s = (mp + jnp.arange(sb)[:, None] * sseq + (sseq - (SSD_CONV - 1)) + tail[None, :]).reshape(-1)
            xbc_p = jnp.take(proj, rows_p, axis=0)[:, SSD_INNER:SSD_INNER + SSD_CONV_DIM]
            xbc_s = jnp.take(proj, rows_s, axis=0)[:, SSD_INNER:SSD_INNER + SSD_CONV_DIM]
            outs["pb"].append(st_p)
            outs["sb"].append(st_s)
            outs["pbc"].append(xbc_p.reshape(pb, SSD_CONV - 1, SSD_CONV_DIM))
            outs["sbc"].append(xbc_s.reshape(sb, SSD_CONV - 1, SSD_CONV_DIM))
            x = _matmul(yb, ssd_w_out, ib, res=x, bm=1152, bn=1024, bk=1024)
            ib += 1
        else:
            hn = _rmsnorm(x, g_mix, i, BF16)
            proj = _matmul(hn, gla_w_in, ic, n_out=GLA_MAIN, bn=1024, w_single_buffer=True)
            w_lr = jnp.pad(gla_w_in[ic][:, GLA_MAIN:], ((0, 0), (0, LANES - GLA_RANK)))[None]
            lrp = _matmul(hn, w_lr, 0, bn=LANES)
            params = (jnp.pad(gla_w_a2[ic], ((0, LANES - GLA_RANK), (0, 0))),
                      gla_b_a[ic].reshape(1, GLA_KEY), gla_norm[ic].reshape(1, GLA_DV))
            ob, s_p = _gla_core(proj, lrp, p_blk0, pb, 1, ROWS, pseq // ROWS, GLA_SUB,
                                jnp.zeros((pb, GLA_HEADS, GLA_DK, GLA_DV), F32), params)
            ob, s_s = _gla_core(proj, lrp, s_blk0, sb, s_nb, sseq, 1, sseq, state_gla[ic], params,
                                out_buf=ob)
            outs["pc"].append(s_p)
            outs["sc"].append(s_s)
            x = _matmul(ob, gla_w_out, ic, res=x)
            ic += 1

        hx = _rmsnorm(x, g_xa, i, BF16)
        qp = _matmul(hx, xa_w_q, i, bn=XA_DIM)
        oc = _xattn(qp, p_mem_k, p_mem_v, i, 0, pb, pseq, bb=1, lq=512)
        oc = _xattn(qp, c_mem_k, c_mem_v, i, mp, sb, sseq, bb=8, lq=sseq, out_buf=oc)
        x = _matmul(oc, xa_w_o, i, res=x)

        hm = _rmsnorm(x, g_mlp, i, BF16)
        hid = _matmul(hm, mlp_w_up, i, mode="relu2", out_dtype=BF16, bn=1024, w_single_buffer=True)
        x = _matmul(hid, mlp_w_down, i, res=x, bm=1152, bn=1024, bk=1024)

    g_fin = norm_final.reshape(1, 1, d)
    y_prompt = _rmsnorm(x, g_fin, 0, F32, row0=0, nrows=mp).reshape(pb, pseq, d)
    y_sample = _rmsnorm(x, g_fin, 0, F32, row0=mp, nrows=ms).reshape(sb, sseq, d)
    shp5 = (DEPTH, -1, MEM_LEN, XA_HEADS, XA_HEAD_DIM)
    return (y_prompt, y_sample,
            jnp.stack(outs["pa_re"]), jnp.stack(outs["pa_im"]), jnp.stack(outs["pb"]),
            jnp.stack(outs["pbc"]), jnp.stack(outs["pc"]),
            p_mem_k.reshape(shp5), p_mem_v.reshape(shp5),
            jnp.stack(outs["sa_re"]), jnp.stack(outs["sa_im"]), jnp.stack(outs["sb"]),
            jnp.stack(outs["sbc"]), jnp.stack(outs["sc"]))
```

```python
import functools
import math

import jax
import jax.numpy as jnp
from jax import lax
from jax.experimental import pallas as pl
from jax.experimental.pallas import tpu as pltpu

F32 = jnp.float32
BF16 = jnp.bfloat16
EPS = 1e-5

D_MODEL = 4096
DEPTH = 4
LANES = 128
ROWS = 128
VMEM_LIMIT = 56 * 1024 * 1024

S5_GROUP = 16
S5_GROUPS = D_MODEL // S5_GROUP
S5_STATE = 64
S5_GPB = LANES // S5_GROUP
S5_BLOCKS = D_MODEL // LANES
S5_SW = S5_GPB * S5_STATE

SSD_INNER = 2 * D_MODEL
SSD_HEAD_DIM = 64
SSD_HEADS = SSD_INNER // SSD_HEAD_DIM
SSD_GROUPS = 8
SSD_HPG = SSD_HEADS // SSD_GROUPS
SSD_STATE = 128
SSD_CONV = 4
SSD_GN = SSD_GROUPS * SSD_STATE
SSD_CONV_DIM = SSD_INNER + 2 * SSD_GN
SSD_IN_DIM = SSD_INNER + SSD_CONV_DIM + SSD_HEADS
SSD_GW = SSD_INNER // SSD_GROUPS

GLA_HEADS = 8
GLA_DK = D_MODEL // 2 // GLA_HEADS
GLA_DV = D_MODEL // GLA_HEADS
GLA_KEY = GLA_HEADS * GLA_DK
GLA_VAL = GLA_HEADS * GLA_DV
GLA_RANK = 16
GLA_TAU = 16.0
GLA_MAIN = 2 * GLA_KEY + 2 * GLA_VAL
GLA_SUB = 16

MEM_LEN = 256
XA_HEADS = 4
XA_HEAD_DIM = 128
XA_DIM = XA_HEADS * XA_HEAD_DIM
MLP_HIDDEN = 4 * D_MODEL


def _cparams(n_axes, vmem=VMEM_LIMIT):
    return pltpu.CompilerParams(dimension_semantics=("arbitrary",) * n_axes,
                                vmem_limit_bytes=vmem)


def _sigmoid(x):
    return 1.0 / (1.0 + jnp.exp(-x))


def _softplus(x):
    return jnp.maximum(x, 0.0) + jnp.log1p(jnp.exp(-jnp.abs(x)))


def _split3(x):
    hi = x.astype(BF16)
    r1 = x - hi.astype(F32)
    mid = r1.astype(BF16)
    lo = (r1 - mid.astype(F32)).astype(BF16)
    return hi, mid, lo


def _dot(a, b):
    return jnp.dot(a, b, preferred_element_type=F32)


def _dot_nt(a, b):
    return lax.dot_general(a, b, (((1,), (1,)), ((), ())), preferred_element_type=F32)


def _dot3_left(m01, x):
    return sum(_dot(m01, p) for p in _split3(x))


def _dot3_right(x, m01):
    return sum(_dot(p, m01) for p in _split3(x))


def _shared_out(kernel_fn, n_in, out_buf):
    if out_buf is None:
        return kernel_fn, [], [], {}

    def body(*refs):
        return kernel_fn(*refs[:n_in], *refs[n_in + 1:])

    return body, [out_buf], [pl.BlockSpec(memory_space=pl.ANY)], {n_in: 0}


def _rmsnorm_kernel(x_ref, g_ref, o_ref):
    x = x_ref[...].astype(F32)
    ms = jnp.mean(x * x, axis=-1, keepdims=True)
    o_ref[...] = (x * lax.rsqrt(ms + EPS) * g_ref[...]).astype(o_ref.dtype)


def _rmsnorm(x, g_stack, layer, out_dtype, bm=256, row0=0, nrows=None):
    m, d = x.shape
    nrows = m if nrows is None else nrows
    assert row0 % bm == 0 and nrows % bm == 0
    blk0 = row0 // bm
    return pl.pallas_call(
        _rmsnorm_kernel,
        grid=(nrows // bm,),
        in_specs=[pl.BlockSpec((bm, d), lambda i: (blk0 + i, 0)),
                  pl.BlockSpec((None, 1, d), lambda i: (layer, 0, 0))],
        out_specs=pl.BlockSpec((bm, d), lambda i: (i, 0)),
        out_shape=jax.ShapeDtypeStruct((nrows, d), out_dtype),
        compiler_params=_cparams(1),
        name="rmsnorm",
    )(x, g_stack)


def _epilogue(mode, acc, acc2):
    if mode == "relu2":
        r = jnp.maximum(acc, 0.0)
        return r * r
    if mode == "glu":
        return acc * _sigmoid(acc2)
    return acc


def _mm_kernel_fullk(*refs, mode, has_res):
    it = iter(refs)
    x_ref = next(it)
    w_ref = next(it)
    w2_ref = next(it) if mode == "glu" else None
    res_ref = next(it) if has_res else None
    o_ref = next(it)
    wbf_ref = next(it)
    wbf2_ref = next(it) if mode == "glu" else None

    @pl.when(pl.program_id(1) == 0)
    def _():
        wbf_ref[...] = w_ref[...].astype(BF16)
        if mode == "glu":
            wbf2_ref[...] = w2_ref[...].astype(BF16)

    x = x_ref[...].astype(BF16)
    acc = _dot(x, wbf_ref[...])
    acc2 = _dot(x, wbf2_ref[...]) if mode == "glu" else None
    out = _epilogue(mode, acc, acc2)
    if has_res:
        out = out + res_ref[...]
    o_ref[...] = out.astype(o_ref.dtype)


def _mm_kernel_splitk(*refs, mode, has_res, nk):
    it = iter(refs)
    x_ref = next(it)
    w_ref = next(it)
    res_ref = next(it) if has_res else None
    o_ref = next(it)
    acc_ref = next(it)
    k = pl.program_id(2)

    @pl.when(k == 0)
    def _():
        acc_ref[...] = jnp.zeros_like(acc_ref)

    acc_ref[...] += _dot(x_ref[...].astype(BF16), w_ref[...].astype(BF16))

    @pl.when(k == nk - 1)
    def _():
        out = _epilogue(mode, acc_ref[...], None)
        if has_res:
            out = out + res_ref[...]
        o_ref[...] = out.astype(o_ref.dtype)


def _matmul(x, w, layer, *, n_out=None, w2=None, res=None, mode="plain", out_dtype=F32,
            bm=1024, bn=512, bk=None):
    m, kdim = x.shape
    n = n_out if n_out is not None else w.shape[2]
    bk = kdim if bk is None else bk
    nk = kdim // bk
    has_res = res is not None
    assert m % bm == 0 and n % bn == 0 and kdim % bk == 0
    if nk == 1:
        grid = (n // bn, m // bm)
        in_specs = [pl.BlockSpec((bm, kdim), lambda j, i: (i, 0)),
                    pl.BlockSpec((None, kdim, bn), lambda j, i: (layer, 0, j))]
        args = [x, w]
        scratch = [pltpu.VMEM((kdim, bn), BF16)]
        if mode == "glu":
            in_specs.append(pl.BlockSpec((None, kdim, bn), lambda j, i: (layer, 0, j)))
            args.append(w2)
            scratch.append(pltpu.VMEM((kdim, bn), BF16))
        if has_res:
            in_specs.append(pl.BlockSpec((bm, bn), lambda j, i: (i, j)))
            args.append(res)
        out_spec = pl.BlockSpec((bm, bn), lambda j, i: (i, j))
        body = functools.partial(_mm_kernel_fullk, mode=mode, has_res=has_res)
        n_axes = 2
    else:
        assert mode != "glu"
        grid = (n // bn, m // bm, nk)
        in_specs = [pl.BlockSpec((bm, bk), lambda j, i, k: (i, k)),
                    pl.BlockSpec((None, bk, bn), lambda j, i, k: (layer, k, j))]
        args = [x, w]
        if has_res:
            in_specs.append(pl.BlockSpec((bm, bn), lambda j, i, k: (i, j)))
            args.append(res)
        out_spec = pl.BlockSpec((bm, bn), lambda j, i, k: (i, j))
        scratch = [pltpu.VMEM((bm, bn), F32)]
        body = functools.partial(_mm_kernel_splitk, mode=mode, has_res=has_res, nk=nk)
        n_axes = 3
    return pl.pallas_call(
        body, grid=grid, in_specs=in_specs, out_specs=out_spec,
        out_shape=jax.ShapeDtypeStruct((m, n), out_dtype),
        scratch_shapes=scratch, compiler_params=_cparams(n_axes),
        name="matmul_" + mode,
    )(*args)


def _xattn_kernel(q_ref, k_ref, v_ref, o_ref, *, bb, lq):
    scale = XA_HEAD_DIM ** -0.5
    for ib in range(bb):
        rows = slice(ib * lq, (ib + 1) * lq)
        for h in range(XA_HEADS):
            sl = slice(h * XA_HEAD_DIM, (h + 1) * XA_HEAD_DIM)
            q = q_ref[rows, sl].astype(BF16)
            mrows = pl.ds(ib * MEM_LEN * XA_HEADS + h, MEM_LEN, stride=XA_HEADS)
            k = k_ref[mrows, :].astype(BF16)
            v = v_ref[mrows, :].astype(BF16)
            s = _dot_nt(q, k) * scale
            p = jnp.exp(s - jnp.max(s, axis=-1, keepdims=True))
            l = jnp.sum(p, axis=-1, keepdims=True)
            o_ref[rows, sl] = (_dot(p.astype(BF16), v) / l).astype(o_ref.dtype)


def _xattn(q, mk, mv, layer, row0, nseq, seq, *, bb, lq, out_buf=None):
    assert bb == 1 or lq == seq
    brows = bb * lq
    assert row0 % brows == 0
    nq = seq // lq
    blk0 = row0 // brows

    def qblk(i, j):
        return (blk0 + i * nq + j, 0)

    in_specs = [pl.BlockSpec((brows, XA_DIM), qblk),
                pl.BlockSpec((None, bb * MEM_LEN * XA_HEADS, XA_HEAD_DIM), lambda i, j: (layer, i, 0)),
                pl.BlockSpec((None, bb * MEM_LEN * XA_HEADS, XA_HEAD_DIM), lambda i, j: (layer, i, 0))]
    body, extra_args, extra_specs, aliases = _shared_out(
        functools.partial(_xattn_kernel, bb=bb, lq=lq), len(in_specs), out_buf)
    return pl.pallas_call(
        body,
        grid=(nseq // bb, nq),
        input_output_aliases=aliases,
        in_specs=in_specs + extra_specs,
        out_specs=pl.BlockSpec((brows, XA_DIM), qblk),
        out_shape=jax.ShapeDtypeStruct(q.shape, F32),
        compiler_params=_cparams(2),
        name="xattn",
    )(q, mk, mv, *extra_args)


def _s5_kernel(u_ref, h0r_ref, h0i_ref, ktr_ref, vc_ref, wc_ref, at_ref, d_ref,
               act_ref, hr_ref, hi_ref, ucat_ref, s_ref, hin_ref, ybuf_ref, bp_ref, cp_ref,
               *, nb, nc, t_len):
    r = nb * nc
    row_g = (lax.broadcasted_iota(jnp.int32, (t_len * LANES, 1), 0) // S5_GROUP) % S5_GPB
    col_h = lax.broadcasted_iota(jnp.int32, (1, S5_SW), 1) // S5_STATE
    for part in range(2):
        pp = vc_ref[0, part]
        tiled = jnp.concatenate([pp] * (S5_SW // LANES), axis=1)
        bp_ref[:, part * S5_SW:(part + 1) * S5_SW] = jnp.where(row_g == col_h, tiled, 0.0).astype(BF16)
    row_h = lax.broadcasted_iota(jnp.int32, (S5_SW, 1), 0) // S5_STATE
    col_g = lax.broadcasted_iota(jnp.int32, (1, LANES), 1) // S5_GROUP
    for t in range(t_len):
        for part in range(2):
            piece = wc_ref[0, t, part * S5_STATE:(part + 1) * S5_STATE, :]
            tiled = jnp.concatenate([piece] * S5_GPB, axis=0)
            cp_ref[t, part * S5_SW:(part + 1) * S5_SW, :] = jnp.where(row_h == col_g, tiled, 0.0).astype(BF16)

    for s in range(t_len):
        ucat_ref[:, s * LANES:(s + 1) * LANES] = u_ref[pl.ds(s, r, stride=t_len), :].astype(BF16)
    half = S5_SW // LANES
    s_all = _dot(ucat_ref[...], bp_ref[...])
    for k in range(2 * half):
        s_ref[k] = s_all[:, k * LANES:(k + 1) * LANES]
    a_sl = [at_ref[0, :, k * LANES:(k + 1) * LANES] for k in range(2 * half)]

    def body(c, carry):
        rows_c = pl.ds(c, nb, stride=nc)
        new = []
        for k in range(half):
            hr, hi = carry[k], carry[half + k]
            hin_ref[k, rows_c, :] = hr
            hin_ref[half + k, rows_c, :] = hi
            new.append((a_sl[k] * hr - a_sl[half + k] * hi + s_ref[k, rows_c, :],
                        a_sl[k] * hi + a_sl[half + k] * hr + s_ref[half + k, rows_c, :]))
        return tuple(n[0] for n in new) + tuple(n[1] for n in new)

    h0 = (tuple(h0r_ref[:, k * LANES:(k + 1) * LANES] for k in range(half))
          + tuple(h0i_ref[:, k * LANES:(k + 1) * LANES] for k in range(half)))
    hfin = lax.fori_loop(0, nc, body, h0)
    for k in range(half):
        hr_ref[:, k * LANES:(k + 1) * LANES] = hfin[k]
        hi_ref[:, k * LANES:(k + 1) * LANES] = hfin[half + k]
    hin = jnp.concatenate([hin_ref[k] for k in range(2 * half)], axis=1).astype(BF16)
    for t in range(t_len):
        y = _dot(hin, cp_ref[t])
        y = y + _dot(ucat_ref[:, :(t + 1) * LANES], ktr_ref[0, (t_len - 1 - t) * LANES:, :])
        ut = u_ref[pl.ds(t, r, stride=t_len), :]
        y = y + d_ref[...] * ut
        ybuf_ref[pl.ds(t, r, stride=t_len), :] = jax.nn.gelu(y)
    act_ref[...] = ybuf_ref[...].astype(act_ref.dtype)


def _s5_prep(a_re, a_im, log_dt, b_re, b_im, c_re, c_im, t_len):
    hp = lax.Precision.HIGHEST
    g, p = a_re.shape
    nblk = g // S5_GPB
    lam_re = a_re.astype(F32)
    lam_im = a_im.astype(F32)
    dt = jnp.exp(log_dt.astype(F32))[:, None]
    zr = lam_re * dt
    zi = lam_im * dt
    ab_re = jnp.exp(zr) * jnp.cos(zi)
    ab_im = jnp.exp(zr) * jnp.sin(zi)
    den = lam_re * lam_re + lam_im * lam_im
    f_re = ((ab_re - 1.0) * lam_re + ab_im * lam_im) / den
    f_im = (ab_im * lam_re - (ab_re - 1.0) * lam_im) / den
    br = b_re.astype(F32)
    bi = b_im.astype(F32)
    bb_re = f_re[..., None] * br - f_im[..., None] * bi
    bb_im = f_re[..., None] * bi + f_im[..., None] * br
    cr = c_re.astype(F32)
    ci = c_im.astype(F32)

    def powers(kk, z_re, z_im):
        mag = jnp.exp(kk * z_re[:, None])
        ang = kk * z_im[:, None]
        return mag * jnp.cos(ang), mag * jnp.sin(ang)

    def rows_gc(x):
        xt = jnp.transpose(x.reshape(nblk, S5_GPB, p, S5_GROUP), (0, 1, 3, 2))
        return jnp.concatenate([xt, xt], axis=-1).reshape(nblk, LANES, 2 * p)

    def rows_g(x):
        xb = jnp.broadcast_to(x.reshape(nblk, S5_GPB, 1, p), (nblk, S5_GPB, S5_GROUP, p))
        return jnp.concatenate([xb, xb], axis=-1).reshape(nblk, LANES, 2 * p)

    kr = (jnp.float32(t_len - 1) - jnp.arange(t_len, dtype=F32)).reshape(1, t_len, 1, 1)
    pr, pi = powers(kr, rows_g(zr), rows_g(zi))
    bbr = rows_gc(bb_re)[:, None]
    bbi = rows_gc(bb_im)[:, None]
    v_re = pr * bbr - pi * bbi
    v_im = pr * bbi + pi * bbr
    bp = jnp.stack([v_re, v_im], axis=1).reshape(nblk, 2, t_len * LANES, LANES)

    def cols_gc(x):
        return jnp.transpose(x.reshape(nblk, S5_GPB, S5_GROUP, p), (0, 3, 1, 2)).reshape(nblk, p, LANES)

    def cols_g(x):
        xt = jnp.transpose(x.reshape(nblk, S5_GPB, p), (0, 2, 1))[..., None]
        return jnp.broadcast_to(xt, (nblk, p, S5_GPB, S5_GROUP)).reshape(nblk, p, LANES)

    kt1 = (1.0 + jnp.arange(t_len, dtype=F32)).reshape(1, t_len, 1, 1)
    qr, qi = powers(kt1, cols_g(zr), cols_g(zi))
    crc = cols_gc(cr)[:, None]
    cic = cols_gc(ci)[:, None]
    cp = jnp.concatenate([crc * qr - cic * qi, -(crc * qi + cic * qr)], axis=2)

    v5r = v_re.reshape(nblk, t_len, S5_GPB, S5_GROUP, 2 * p)[..., :p]
    v5i = v_im.reshape(nblk, t_len, S5_GPB, S5_GROUP, 2 * p)[..., :p]
    cr4 = cr.reshape(nblk, S5_GPB, S5_GROUP, p)
    ci4 = ci.reshape(nblk, S5_GPB, S5_GROUP, p)
    ktt = (jnp.einsum('jmgdp,jgcp->jmgdc', v5r, cr4, precision=hp)
           - jnp.einsum('jmgdp,jgcp->jmgdc', v5i, ci4, precision=hp))
    eye = jnp.eye(S5_GPB, dtype=F32)
    ktr = (ktt[:, :, :, :, None, :] * eye.reshape(1, 1, S5_GPB, 1, S5_GPB, 1)).astype(BF16)
    ktr = ktr.reshape(nblk, t_len * LANES, LANES)

    tt = jnp.float32(t_len)
    at = jnp.concatenate([(jnp.exp(tt * zr) * jnp.cos(tt * zi)).reshape(nblk, 1, S5_SW),
                          (jnp.exp(tt * zr) * jnp.sin(tt * zi)).reshape(nblk, 1, S5_SW)], axis=2)
    return ktr, bp, cp, at


def _s5_core(u, row0, nb, seq, t_len, h0_re, h0_im, ops, d_skip, out_buf=None):
    ktr, bp, cp, at = ops
    nc = seq // t_len
    r = nb * nc
    nrows = nb * seq
    assert row0 % nrows == 0
    rb = row0 // nrows
    h0r = h0_re.reshape(nb, S5_GROUPS * S5_STATE)
    h0i = h0_im.reshape(nb, S5_GROUPS * S5_STATE)
    out_shapes = (jax.ShapeDtypeStruct((u.shape[0], D_MODEL), BF16),
                  jax.ShapeDtypeStruct((nb, S5_GROUPS * S5_STATE), F32),
                  jax.ShapeDtypeStruct((nb, S5_GROUPS * S5_STATE), F32))
    body, extra_args, extra_specs, aliases = _shared_out(
        functools.partial(_s5_kernel, nb=nb, nc=nc, t_len=t_len), 8, out_buf)
    act, hr, hi = pl.pallas_call(
        body,
        grid=(S5_BLOCKS,),
        input_output_aliases=aliases,
        in_specs=[pl.BlockSpec((nrows, LANES), lambda j: (rb, j)),
                  pl.BlockSpec((nb, S5_SW), lambda j: (0, j)),
                  pl.BlockSpec((nb, S5_SW), lambda j: (0, j)),
                  pl.BlockSpec((1, t_len * LANES, LANES), lambda j: (j, 0, 0)),
                  pl.BlockSpec((1, 2, t_len * LANES, LANES), lambda j: (j, 0, 0, 0)),
                  pl.BlockSpec((1, t_len, 2 * S5_STATE, LANES), lambda j: (j, 0, 0, 0)),
                  pl.BlockSpec((1, 1, 2 * S5_SW), lambda j: (j, 0, 0)),
                  pl.BlockSpec((1, LANES), lambda j: (0, j))] + extra_specs,
        out_specs=(pl.BlockSpec((nrows, LANES), lambda j: (rb, j)),
                   pl.BlockSpec((nb, S5_SW), lambda j: (0, j)),
                   pl.BlockSpec((nb, S5_SW), lambda j: (0, j))),
        out_shape=out_shapes,
        scratch_shapes=[pltpu.VMEM((r, t_len * LANES), BF16),
                        pltpu.VMEM((2 * S5_SW // LANES, r, LANES), F32),
                        pltpu.VMEM((2 * S5_SW // LANES, r, LANES), F32),
                        pltpu.VMEM((nrows, LANES), F32),
                        pltpu.VMEM((t_len * LANES, 2 * S5_SW), BF16),
                        pltpu.VMEM((t_len, 2 * S5_SW, LANES), BF16)],
        compiler_params=_cparams(1),
        name="s5_core",
    )(u, h0r, h0i, ktr, bp, cp, at, d_skip.reshape(1, D_MODEL), *extra_args)
    return (act, hr.reshape(nb, S5_GROUPS, S5_STATE), hi.reshape(nb, S5_GROUPS, S5_STATE))


def _ssd_kernel(z_ref, x_ref, b_ref, c_ref, dt_ref, dtt_ref,
                cwx_ref, cwb_ref, cwc_ref, cbx_ref, cbb_ref, cbc_ref,
                csx_ref, csb_ref, csc_ref,
                dtb_ref, dtbt_ref, alog_ref, alogt_ref, dexp_ref, norm_ref, st0_ref,
                y_ref, st_ref, xbuf, bbuf, cbuf, *, nb, qt):
    rows = nb * qt
    c = pl.program_id(2)
    pad = 8
    hist = SSD_CONV - 1

    @pl.when(c == 0)
    def _():
        st_ref[...] = st0_ref[...]
        xbuf[:, pad - hist:pad, :] = csx_ref[...]
        bbuf[:, pad - hist:pad, :] = csb_ref[...]
        cbuf[:, pad - hist:pad, :] = csc_ref[...]

    def conv(buf, raw_ref, w_ref, bias_ref, width):
        buf[:, pad:, :] = raw_ref[...].reshape(nb, qt, width)
        acc = bias_ref[...].reshape(1, 1, width)
        for k in range(SSD_CONV):
            acc = acc + buf[:, pad - hist + k:pad - hist + k + qt, :] * w_ref[k:k + 1, :].reshape(1, 1, width)
        tail = buf[:, pad + qt - hist:pad + qt, :]
        buf[:, pad - hist:pad, :] = tail
        acc = acc * _sigmoid(acc)
        return acc.reshape(rows, width)

    xs = conv(xbuf, x_ref, cwx_ref, cbx_ref, SSD_GW)
    bs = conv(bbuf, b_ref, cwb_ref, cbb_ref, SSD_STATE).astype(BF16)
    cs = conv(cbuf, c_ref, cwc_ref, cbc_ref, SSD_STATE).astype(BF16)

    dt = _softplus(dt_ref[0] + dtb_ref[0])
    dtt = _softplus(dtt_ref[0] + dtbt_ref[0])
    da = dt * (-jnp.exp(alog_ref[0]))
    dat = dtt * (-jnp.exp(alogt_ref[0]))

    ri = lax.broadcasted_iota(jnp.int32, (rows, rows), 0)
    ci = lax.broadcasted_iota(jnp.int32, (rows, rows), 1)
    same = (ri // qt) == (ci // qt)
    tri = same & (ri >= ci)
    tri_bf = tri.astype(BF16)
    trit_bf = (same & (ri <= ci)).astype(BF16)
    same_bf = same.astype(BF16)
    cum = _dot3_left(tri_bf, da)
    cumt = _dot3_right(dat, trit_bf)
    tot = _dot3_left(same_bf, da)
    cb = _dot_nt(cs, bs)

    lane = lax.broadcasted_iota(jnp.int32, (1, LANES), 1)
    lo = lane < SSD_HEAD_DIM
    srow = lax.broadcasted_iota(jnp.int32, (LANES, 1), 0) < SSD_HEAD_DIM
    row_seq = lax.broadcasted_iota(jnp.int32, (rows, 1), 0) // qt
    neg_inf = jnp.float32(-jnp.inf)
    bs_seq = [bs if nb == 1 else jnp.where(row_seq == ib, bs, jnp.zeros_like(bs)) for ib in range(nb)]

    ys = []
    for p in range(SSD_HPG // 2):
        ra, rb = 2 * p, 2 * p + 1
        ws = []
        for r in (ra, rb):
            seg = cum[:, r:r + 1] - cumt[r:r + 1, :]
            dec = jnp.exp(jnp.where(tri, seg, neg_inf))
            ws.append((cb * dec * dtt[r:r + 1, :]).astype(BF16))
        xp = xs[:, p * LANES:(p + 1) * LANES]
        y = (_dot(ws[0], jnp.where(lo, xp, 0.0).astype(BF16))
             + _dot(ws[1], jnp.where(lo, 0.0, xp).astype(BF16)))
        ecol = jnp.where(lo, jnp.exp(cum[:, ra:ra + 1]), jnp.exp(cum[:, rb:rb + 1]))
        inter = []
        for ib in range(nb):
            st = st_ref[ib, 0, p * LANES:(p + 1) * LANES, :].astype(BF16)
            inter.append(_dot_nt(cs[ib * qt:(ib + 1) * qt, :], st))
        inter = inter[0] if nb == 1 else jnp.concatenate(inter, axis=0)
        y = y + inter * ecol + dexp_ref[:, p * LANES:(p + 1) * LANES] * xp
        ys.append(y)

        sc = jnp.where(lo, jnp.exp(tot[:, ra:ra + 1] - cum[:, ra:ra + 1]) * dt[:, ra:ra + 1],
                       jnp.exp(tot[:, rb:rb + 1] - cum[:, rb:rb + 1]) * dt[:, rb:rb + 1])
        xwt = jnp.transpose(xp * sc).astype(BF16)
        for ib in range(nb):
            upd = _dot(xwt, bs_seq[ib])
            r0 = ib * qt
            dcol = jnp.where(srow, jnp.exp(tot[r0:r0 + 1, ra:ra + 1]), jnp.exp(tot[r0:r0 + 1, rb:rb + 1]))
            old = st_ref[ib, 0, p * LANES:(p + 1) * LANES, :]
            st_ref[ib, 0, p * LANES:(p + 1) * LANES, :] = old * dcol + upd

    y = jnp.concatenate(ys, axis=1)
    zg = z_ref[...]
    y = y * (zg * _sigmoid(zg))
    ms = jnp.mean(y * y, axis=-1, keepdims=True)
    y_ref[...] = (y * lax.rsqrt(ms + EPS) * norm_ref[...]).astype(y_ref.dtype)


def _ssd_core(proj, dtp, dtt, row_blk0, nseq, nb, qt, nc, st0, conv0, params, out_buf=None):
    cw, cbias, dtb, dtbt, alog, alogt, dexp, norm = params
    nbb = nseq // nb
    xoff = SSD_INNER // SSD_GW
    boff = (SSD_INNER + SSD_INNER) // SSD_STATE
    coff = boff + SSD_GN // SSD_STATE
    cwb_off = SSD_INNER // SSD_STATE
    cwc_off = cwb_off + SSD_GN // SSD_STATE

    def rblk(i, g, c):
        return row_blk0 + i * nc + c

    st0 = st0.reshape(nseq, SSD_GROUPS, SSD_GW, SSD_STATE)
    hist = SSD_CONV - 1
    in_specs = [
        pl.BlockSpec((ROWS, SSD_GW), lambda i, g, c: (rblk(i, g, c), g)),
        pl.BlockSpec((ROWS, SSD_GW), lambda i, g, c: (rblk(i, g, c), xoff + g)),
        pl.BlockSpec((ROWS, SSD_STATE), lambda i, g, c: (rblk(i, g, c), boff + g)),
        pl.BlockSpec((ROWS, SSD_STATE), lambda i, g, c: (rblk(i, g, c), coff + g)),
        pl.BlockSpec((1, ROWS, LANES), lambda i, g, c: (g, rblk(i, g, c), 0)),
        pl.BlockSpec((1, SSD_HPG, ROWS), lambda i, g, c: (g, 0, rblk(i, g, c))),
        pl.BlockSpec((SSD_CONV, SSD_GW), lambda i, g, c: (0, g)),
        pl.BlockSpec((SSD_CONV, SSD_STATE), lambda i, g, c: (0, cwb_off + g)),
        pl.BlockSpec((SSD_CONV, SSD_STATE), lambda i, g, c: (0, cwc_off + g)),
        pl.BlockSpec((1, SSD_GW), lambda i, g, c: (0, g)),
        pl.BlockSpec((1, SSD_STATE), lambda i, g, c: (0, cwb_off + g)),
        pl.BlockSpec((1, SSD_STATE), lambda i, g, c: (0, cwc_off + g)),
        pl.BlockSpec((nb, hist, SSD_GW), lambda i, g, c: (i, 0, g)),
        pl.BlockSpec((nb, hist, SSD_STATE), lambda i, g, c: (i, 0, cwb_off + g)),
        pl.BlockSpec((nb, hist, SSD_STATE), lambda i, g, c: (i, 0, cwc_off + g)),
        pl.BlockSpec((1, 1, LANES), lambda i, g, c: (g, 0, 0)),
        pl.BlockSpec((1, SSD_HPG, 1), lambda i, g, c: (g, 0, 0)),
        pl.BlockSpec((1, 1, LANES), lambda i, g, c: (g, 0, 0)),
        pl.BlockSpec((1, SSD_HPG, 1), lambda i, g, c: (g, 0, 0)),
        pl.BlockSpec((1, SSD_GW), lambda i, g, c: (0, g)),
        pl.BlockSpec((1, SSD_GW), lambda i, g, c: (0, g)),
        pl.BlockSpec((nb, 1, SSD_GW, SSD_STATE), lambda i, g, c: (i, g, 0, 0)),
    ]
    out_specs = (pl.BlockSpec((ROWS, SSD_GW), lambda i, g, c: (rblk(i, g, c), g)),
                 pl.BlockSpec((nb, 1, SSD_GW, SSD_STATE), lambda i, g, c: (i, g, 0, 0)))
    out_shape = (jax.ShapeDtypeStruct((proj.shape[0], SSD_INNER), BF16),
                 jax.ShapeDtypeStruct((nseq, SSD_GROUPS, SSD_GW, SSD_STATE), F32))
    body, extra_args, extra_specs, aliases = _shared_out(
        functools.partial(_ssd_kernel, nb=nb, qt=qt), len(in_specs), out_buf)
    y, st = pl.pallas_call(
        body,
        grid=(nbb, SSD_GROUPS, nc),
        input_output_aliases=aliases,
        in_specs=in_specs + extra_specs, out_specs=out_specs, out_shape=out_shape,
        scratch_shapes=[pltpu.VMEM((nb, 8 + qt, SSD_GW), F32),
                        pltpu.VMEM((nb, 8 + qt, SSD_STATE), F32),
                        pltpu.VMEM((nb, 8 + qt, SSD_STATE), F32)],
        compiler_params=_cparams(3),
        name="ssd_core",
    )(proj, proj, proj, proj, dtp, dtt, cw, cw, cw, cbias, cbias, cbias,
      conv0, conv0, conv0, dtb, dtbt, alog, alogt, dexp, norm, st0, *extra_args)
    return y, st.reshape(nseq, SSD_HEADS, SSD_HEAD_DIM, SSD_STATE)


def _gla_kernel(q_ref, k_ref, v_ref, r_ref, lr_ref, wa2_ref, ba_ref, norm_ref, s0_ref,
                o_ref, s_ref, cum_s, q_s, k_s, att_s, *, nb, qt, wq):
    rows = nb * qt
    c = pl.program_id(2)

    @pl.when(c == 0)
    def _():
        s_ref[...] = s0_ref[...]

    qv = q_ref[...] * (GLA_DK ** -0.5)
    kv = k_ref[...]
    v_bf = v_ref[...].astype(BF16)
    gate = _dot(lr_ref[...].astype(BF16), wa2_ref[...].astype(BF16)) + ba_ref[...]
    g = -_softplus(-gate) / GLA_TAU

    ri = lax.broadcasted_iota(jnp.int32, (rows, rows), 0)
    ci = lax.broadcasted_iota(jnp.int32, (rows, rows), 1)
    same = (ri // qt) == (ci // qt)
    tri_bf = (same & (ri >= ci)).astype(BF16)
    same_bf = same.astype(BF16)
    cum = _dot3_left(tri_bf, g)
    tot = _dot3_left(same_bf, g)
    cum_s[...] = cum
    q_s[...] = qv
    k_s[...] = kv

    qe = (qv * jnp.exp(cum)).astype(BF16)
    o_parts = [_dot(qe[ib * qt:(ib + 1) * qt, :], s_ref[ib, 0].astype(BF16)) for ib in range(nb)]
    o = o_parts[0] if nb == 1 else jnp.concatenate(o_parts, axis=0)

    neg_inf = jnp.float32(-jnp.inf)
    att_s[...] = jnp.zeros_like(att_s)
    if qt > wq:
        assert nb == 1
        rowi = lax.broadcasted_iota(jnp.int32, (rows, 1), 0)
        for blk in range(1, qt // wq):
            lo_r, hi_r = blk * wq, (blk + 1) * wq
            c0 = cum[lo_r - 1:lo_r, :]
            q_blk = (qv[lo_r:hi_r, :] * jnp.exp(cum[lo_r:hi_r, :] - c0)).astype(BF16)
            k_all = (kv * jnp.exp(jnp.where(rowi < lo_r, c0 - cum, neg_inf))).astype(BF16)
            att_s[lo_r:hi_r, :] = _dot_nt(q_blk, k_all)

    lane = lax.broadcasted_iota(jnp.int32, (1, rows), 1)
    wrow = lax.broadcasted_iota(jnp.int32, (wq, 1), 0)

    def window(w, carry):
        w0 = pl.multiple_of(w * wq, wq)
        cw = cum_s[pl.ds(w0, wq), :]
        qw = q_s[pl.ds(w0, wq), :]
        kw = k_s[pl.ds(w0, wq), :]
        blk = att_s[pl.ds(w0, wq), :]
        for jj in range(wq):
            e = jnp.exp(jnp.where(wrow >= jj, cw - cw[jj:jj + 1, :], neg_inf))
            col = jnp.sum(qw * e * kw[jj:jj + 1, :], axis=-1, keepdims=True)
            blk = jnp.where(lane == w0 + jj, col, blk)
        att_s[pl.ds(w0, wq), :] = blk
        return carry

    lax.fori_loop(0, rows // wq, window, 0)
    o = o + _dot(att_s[...].astype(BF16), v_bf)

    kdt = jnp.transpose(kv * jnp.exp(tot - cum)).astype(BF16)
    gt = jnp.transpose(g)
    rsel = lax.broadcasted_iota(jnp.int32, (rows, LANES), 0) // qt
    row_seq = lax.broadcasted_iota(jnp.int32, (rows, 1), 0) // qt
    for ib in range(nb):
        sel = (rsel == ib).astype(BF16)
        dec = jnp.exp(_dot3_right(gt, sel))
        dec = jnp.concatenate([dec] * (GLA_DV // LANES), axis=1)
        v_ib = v_bf if nb == 1 else jnp.where(row_seq == ib, v_bf, jnp.zeros_like(v_bf))
        s_ref[ib, 0] = s_ref[ib, 0] * dec + _dot(kdt, v_ib)

    ms = jnp.mean(o * o, axis=-1, keepdims=True)
    o = o * lax.rsqrt(ms + EPS) * norm_ref[...]
    rr = r_ref[...]
    o_ref[...] = (o * (rr * _sigmoid(rr))).astype(o_ref.dtype)


def _gla_core(proj, lrp, row_blk0, nseq, nb, qt, nc, wq, s0, params, out_buf=None):
    wa2p, ba, norm = params
    nbb = nseq // nb
    koff = GLA_KEY // GLA_DK
    voff = 2 * GLA_KEY // GLA_DV
    roff = voff + GLA_VAL // GLA_DV

    def rblk(i, h, c):
        return row_blk0 + i * nc + c

    in_specs = [
        pl.BlockSpec((ROWS, GLA_DK), lambda i, h, c: (rblk(i, h, c), h)),
        pl.BlockSpec((ROWS, GLA_DK), lambda i, h, c: (rblk(i, h, c), koff + h)),
        pl.BlockSpec((ROWS, GLA_DV), lambda i, h, c: (rblk(i, h, c), voff + h)),
        pl.BlockSpec((ROWS, GLA_DV), lambda i, h, c: (rblk(i, h, c), roff + h)),
        pl.BlockSpec((ROWS, LANES), lambda i, h, c: (rblk(i, h, c), 0)),
        pl.BlockSpec((LANES, GLA_DK), lambda i, h, c: (0, h)),
        pl.BlockSpec((1, GLA_DK), lambda i, h, c: (0, h)),
        pl.BlockSpec((1, GLA_DV), lambda i, h, c: (0, 0)),
        pl.BlockSpec((nb, 1, GLA_DK, GLA_DV), lambda i, h, c: (i, h, 0, 0)),
    ]
    out_specs = (pl.BlockSpec((ROWS, GLA_DV), lambda i, h, c: (rblk(i, h, c), h)),
                 pl.BlockSpec((nb, 1, GLA_DK, GLA_DV), lambda i, h, c: (i, h, 0, 0)))
    out_shape = (jax.ShapeDtypeStruct((proj.shape[0], GLA_VAL), BF16),
                 jax.ShapeDtypeStruct((nseq, GLA_HEADS, GLA_DK, GLA_DV), F32))
    body, extra_args, extra_specs, aliases = _shared_out(
        functools.partial(_gla_kernel, nb=nb, qt=qt, wq=wq), len(in_specs), out_buf)
    return pl.pallas_call(
        body,
        grid=(nbb, GLA_HEADS, nc),
        input_output_aliases=aliases,
        in_specs=in_specs + extra_specs, out_specs=out_specs, out_shape=out_shape,
        scratch_shapes=[pltpu.VMEM((ROWS, GLA_DK), F32), pltpu.VMEM((ROWS, GLA_DK), F32),
                        pltpu.VMEM((ROWS, GLA_DK), F32), pltpu.VMEM((ROWS, ROWS), F32)],
        compiler_params=_cparams(3),
        name="gla_core",
    )(proj, proj, proj, proj, lrp, wa2p, ba, norm, s0, *extra_args)


def kernel(x_prompt, x_sample, state_s5_re, state_s5_im, state_ssd, state_ssd_conv, state_gla, cache_mem_k, cache_mem_v, mem_prompt, norm_mix, norm_xattn, norm_mem, norm_mlp, norm_final, xa_w_q, xa_w_k, xa_w_v, xa_w_o, mlp_w_up, mlp_w_down, s5_a_re, s5_a_im, s5_log_dt, s5_b_re, s5_b_im, s5_c_re, s5_c_im, s5_d, s5_glu_w1, s5_glu_w2, ssd_w_in, ssd_conv_w, ssd_conv_b, ssd_dt_bias, ssd_a_log, ssd_d, ssd_norm, ssd_w_out, gla_w_in, gla_w_a2, gla_b_a, gla_norm, gla_w_out):
    pb, pseq, d = x_prompt.shape
    sb, sseq, _ = x_sample.shape
    mp = pb * pseq
    ms = sb * sseq
    assert mp % ROWS == 0 and ms % ROWS == 0 and pseq % ROWS == 0 and ROWS % sseq == 0
    x = jnp.concatenate([x_prompt.reshape(mp, d), x_sample.reshape(ms, d)], axis=0)
    m = mp + ms
    p_blk0 = 0
    s_blk0 = mp // ROWS
    s_nb = ROWS // sseq

    g_mix = norm_mix.reshape(DEPTH, 1, d)
    g_xa = norm_xattn.reshape(DEPTH, 1, d)
    g_mem = norm_mem.reshape(DEPTH, 1, d)
    g_mlp = norm_mlp.reshape(DEPTH, 1, d)

    mem2 = mem_prompt.reshape(pb * MEM_LEN, d)
    pk, pv = [], []
    for i in range(DEPTH):
        mn = _rmsnorm(mem2, g_mem, i, BF16)
        pk.append(_matmul(mn, xa_w_k, i, bm=512, bn=XA_DIM))
        pv.append(_matmul(mn, xa_w_v, i, bm=512, bn=XA_DIM))
    p_mem_k = jnp.stack(pk).reshape(DEPTH, pb * MEM_LEN * XA_HEADS, XA_HEAD_DIM)
    p_mem_v = jnp.stack(pv).reshape(DEPTH, pb * MEM_LEN * XA_HEADS, XA_HEAD_DIM)
    c_mem_k = cache_mem_k.reshape(DEPTH, sb * MEM_LEN * XA_HEADS, XA_HEAD_DIM)
    c_mem_v = cache_mem_v.reshape(DEPTH, sb * MEM_LEN * XA_HEADS, XA_HEAD_DIM)

    outs = {k: [] for k in ("pa_re", "pa_im", "pb", "pbc", "pc", "sa_re", "sa_im", "sb", "sbc", "sc")}
    ia = ib = ic = 0
    for i in range(DEPTH):
        kind = i % 3
        if kind == 0:
            hn = _rmsnorm(x, g_mix, i, F32)
            ops_p = _s5_prep(s5_a_re[ia], s5_a_im[ia], s5_log_dt[ia], s5_b_re[ia], s5_b_im[ia],
                             s5_c_re[ia], s5_c_im[ia], 16)
            ops_s = _s5_prep(s5_a_re[ia], s5_a_im[ia], s5_log_dt[ia], s5_b_re[ia], s5_b_im[ia],
                             s5_c_re[ia], s5_c_im[ia], sseq)
            zero = jnp.zeros((pb, S5_GROUPS, S5_STATE), F32)
            act, hr_p, hi_p = _s5_core(hn, 0, pb, pseq, 16, zero, zero, ops_p, s5_d[ia])
            act, hr_s, hi_s = _s5_core(hn, mp, sb, sseq, sseq, state_s5_re[ia], state_s5_im[ia],
                                       ops_s, s5_d[ia], out_buf=act)
            outs["pa_re"].append(hr_p)
            outs["pa_im"].append(hi_p)
            outs["sa_re"].append(hr_s)
            outs["sa_im"].append(hi_s)
            x = _matmul(act, s5_glu_w1, ia, w2=s5_glu_w2, res=x, mode="glu", bn=256)
            ia += 1
        elif kind == 1:
            hn = _rmsnorm(x, g_mix, i, BF16)
            proj = _matmul(hn, ssd_w_in, ib, bn=640)
            dt_raw = proj[:, SSD_INNER + SSD_CONV_DIM:].reshape(m, SSD_GROUPS, SSD_HPG)
            dtt = jnp.transpose(dt_raw, (1, 2, 0))
            dtp = jnp.pad(jnp.transpose(dt_raw, (1, 0, 2)), ((0, 0), (0, 0), (0, LANES - SSD_HPG)))
            pad_h = ((0, 0), (0, 0), (0, LANES - SSD_HPG))
            params = (ssd_conv_w[ib], ssd_conv_b[ib].reshape(1, SSD_CONV_DIM),
                      jnp.pad(ssd_dt_bias[ib].reshape(SSD_GROUPS, 1, SSD_HPG), pad_h),
                      ssd_dt_bias[ib].reshape(SSD_GROUPS, SSD_HPG, 1),
                      jnp.pad(ssd_a_log[ib].reshape(SSD_GROUPS, 1, SSD_HPG), pad_h),
                      ssd_a_log[ib].reshape(SSD_GROUPS, SSD_HPG, 1),
                      jnp.repeat(ssd_d[ib], SSD_HEAD_DIM).reshape(1, SSD_INNER),
                      ssd_norm[ib].reshape(1, SSD_INNER))
            yb, st_p = _ssd_core(proj, dtp, dtt, p_blk0, pb, 1, ROWS, pseq // ROWS,
                                 jnp.zeros((pb, SSD_HEADS, SSD_HEAD_DIM, SSD_STATE), F32),
                                 jnp.zeros((pb, SSD_CONV - 1, SSD_CONV_DIM), F32), params)
            yb, st_s = _ssd_core(proj, dtp, dtt, s_blk0, sb, s_nb, sseq, 1,
                                 state_ssd[ib], state_ssd_conv[ib], params, out_buf=yb)
            tail = jnp.arange(SSD_CONV - 1)
            rows_p = (jnp.arange(pb)[:, None] * pseq + (pseq - (SSD_CONV - 1)) + tail[None, :]).reshape(-1)
            rows_s = (mp + jnp.arange(sb)[:, None] * sseq + (sseq - (SSD_CONV - 1)) + tail[None, :]).reshape(-1)
            xbc_p = jnp.take(proj, rows_p, axis=0)[:, SSD_INNER:SSD_INNER + SSD_CONV_DIM]
            xbc_s = jnp.take(proj, rows_s, axis=0)[:, SSD_INNER:SSD_INNER + SSD_CONV_DIM]
            outs["pb"].append(st_p)
            outs["sb"].append(st_s)
            outs["pbc"].append(xbc_p.reshape(pb, SSD_CONV - 1, SSD_CONV_DIM))
            outs["sbc"].append(xbc_s.reshape(sb, SSD_CONV - 1, SSD_CONV_DIM))
            x = _matmul(yb, ssd_w_out, ib, res=x, bm=1152, bn=1024, bk=2048)
            ib += 1
        else:
            hn = _rmsnorm(x, g_mix, i, BF16)
            proj = _matmul(hn, gla_w_in, ic, n_out=GLA_MAIN)
            w_lr = jnp.pad(gla_w_in[ic][:, GLA_MAIN:], ((0, 0), (0, LANES - GLA_RANK)))[None]
            lrp = _matmul(hn, w_lr, 0, bn=LANES)
            params = (jnp.pad(gla_w_a2[ic], ((0, LANES - GLA_RANK), (0, 0))),
                      gla_b_a[ic].reshape(1, GLA_KEY), gla_norm[ic].reshape(1, GLA_DV))
            ob, s_p = _gla_core(proj, lrp, p_blk0, pb, 1, ROWS, pseq // ROWS, GLA_SUB,
                                jnp.zeros((pb, GLA_HEADS, GLA_DK, GLA_DV), F32), params)
            ob, s_s = _gla_core(proj, lrp, s_blk0, sb, s_nb, sseq, 1, sseq, state_gla[ic], params,
                                out_buf=ob)
            outs["pc"].append(s_p)
            outs["sc"].append(s_s)
            x = _matmul(ob, gla_w_out, ic, res=x)
            ic += 1

        hx = _rmsnorm(x, g_xa, i, BF16)
        qp = _matmul(hx, xa_w_q, i, bn=XA_DIM)
        oc = _xattn(qp, p_mem_k, p_mem_v, i, 0, pb, pseq, bb=1, lq=512)
        oc = _xattn(qp, c_mem_k, c_mem_v, i, mp, sb, sseq, bb=8, lq=sseq, out_buf=oc)
        x = _matmul(oc, xa_w_o, i, res=x)

        hm = _rmsnorm(x, g_mlp, i, BF16)
        hid = _matmul(hm, mlp_w_up, i, mode="relu2", out_dtype=BF16)
        x = _matmul(hid, mlp_w_down, i, res=x, bm=1152, bn=1024, bk=2048)

    g_fin = norm_final.reshape(1, 1, d)
    y_prompt = _rmsnorm(x, g_fin, 0, F32, row0=0, nrows=mp).reshape(pb, pseq, d)
    y_sample = _rmsnorm(x, g_fin, 0, F32, row0=mp, nrows=ms).reshape(sb, sseq, d)
    shp5 = (DEPTH, -1, MEM_LEN, XA_HEADS, XA_HEAD_DIM)
    return (y_prompt, y_sample,
            jnp.stack(outs["pa_re"]), jnp.stack(outs["pa_im"]), jnp.stack(outs["pb"]),
            jnp.stack(outs["pbc"]), jnp.stack(outs["pc"]),
            p_mem_k.reshape(shp5), p_mem_v.reshape(shp5),
            jnp.stack(outs["sa_re"]), jnp.stack(outs["sa_im"]), jnp.stack(outs["sb"]),
            jnp.stack(outs["sbc"]), jnp.stack(outs["sc"]))
```
